```python
import math
import jax, jax.numpy as jnp
from jax import lax
import numpy as np

D_MODEL = 1024
BATCH = 16
SEQ = 4096
DEPTH = 1
DEC_BATCH = 8
DEC_SEQ = 64
PAST_LEN = 1024

CHUNK = 64
Q_BLOCK = 128
EPS = 1e-6
NEG = -1e30
D_MLSTM = D_MODEL
H_M = 4
DH_M = D_MLSTM // H_M
CONV_W = 4
H_A = 8
NOPE = 128
ROPE = 64
V_DIM = 128
Q_LORA = 384
KV_LORA = 256
D_ATT = H_A * V_DIM
ROPE_THETA = 10000.0
ATT_SCALE = (NOPE + ROPE) ** -0.5
SPLITS = (D_MLSTM, D_MLSTM, H_M, H_M, D_MLSTM, D_MLSTM, Q_LORA, KV_LORA, ROPE, D_ATT, D_MODEL, D_MODEL)
N_IN = D_MLSTM * 4 + H_M * 2 + Q_LORA + KV_LORA + ROPE + D_ATT + D_MODEL * 2

kernel_name = "mlstm_mla_gated_hybrid_stream_step"


def rmsnorm(x, g):
    xf = x.astype(jnp.float32)
    r = lax.rsqrt(jnp.mean(xf * xf, axis=-1, keepdims=True) + EPS)
    return (xf * r).astype(x.dtype) * g


def rope(x, pos):
    half = ROPE // 2
    inv = jnp.power(ROPE_THETA, -jnp.arange(half, dtype=jnp.float32) / half)
    ang = pos.astype(jnp.float32)[:, None] * inv[None, :]
    shape = (pos.shape[0],) + (1,) * (x.ndim - 3) + (half,)
    cos = jnp.cos(ang).reshape(shape)
    sin = jnp.sin(ang).reshape(shape)
    xf = x.astype(jnp.float32)
    x1, x2 = xf[..., :half], xf[..., half:]
    return jnp.concatenate([x1 * cos - x2 * sin, x2 * cos + x1 * sin], axis=-1).astype(x.dtype)


def split_cols(proj):
    out, start = [], 0
    for w in SPLITS:
        out.append(proj[..., start:start + w])
        start += w
    return out


def causal_conv(u, buf, w, b):
    L = u.shape[1]
    full = jnp.concatenate([buf.astype(u.dtype), u], axis=1)
    out = b + sum(full[:, j:j + L, :] * w[j] for j in range(CONV_W))
    return out, full[:, -(CONV_W - 1):, :]


def mlstm_chunk(carry, xs):
    C0, n0, m0 = carry
    q, k, v, ig, lf = xs
    f32 = jnp.float32
    qf, kf, vf = q.astype(f32), k.astype(f32), v.astype(f32)
    L = q.shape[2]
    b = jnp.cumsum(lf, axis=-1)
    causal = jnp.tril(jnp.ones((L, L), dtype=bool))
    dlog = b[..., :, None] - b[..., None, :] + ig[..., None, :]
    dlog = jnp.where(causal, dlog, NEG)
    g = b + m0[..., None]
    m = jnp.maximum(g, jnp.max(dlog, axis=-1))
    w_intra = jnp.exp(dlog - m[..., None])
    w_inter = jnp.exp(g - m)
    s = jnp.einsum('bhtd,bhsd->bhts', qf, kf) * w_intra
    num = w_inter[..., None] * jnp.einsum('bhtd,bhde->bhte', qf, C0) + jnp.einsum('bhts,bhse->bhte', s, vf)
    den = w_inter * jnp.einsum('bhtd,bhd->bht', qf, n0) + jnp.sum(s, axis=-1)
    h = num / jnp.maximum(jnp.abs(den), jnp.exp(-m))[..., None]
    m_last = m[..., -1]
    w_state = jnp.exp(b[..., -1:] - b + ig - m_last[..., None])
    decay = jnp.exp(b[..., -1] + m0 - m_last)
    C1 = decay[..., None, None] * C0 + jnp.einsum('bhsd,bhse->bhde', kf * w_state[..., None], vf)
    n1 = decay[..., None] * n0 + jnp.einsum('bhs,bhsd->bhd', w_state, kf)
    return (C1, n1, m_last), h


def mlstm_seq(q, k, v, ig, lf, C0, n0, m0):
    B, H, L, _ = q.shape
    if L <= CHUNK:
        (C1, n1, m1), h = mlstm_chunk((C0, n0, m0), (q, k, v, ig, lf))
        return h, C1, n1, m1
    nc = L // CHUNK

    def to_chunks(a):
        return jnp.moveaxis(a.reshape((B, H, nc, CHUNK) + a.shape[3:]), 2, 0)

    xs = (to_chunks(q), to_chunks(k), to_chunks(v), to_chunks(ig), to_chunks(lf))
    (C1, n1, m1), hc = lax.scan(mlstm_chunk, (C0, n0, m0), xs)
    h = jnp.moveaxis(hc, 0, 2).reshape(B, H, L, -1)
    return h, C1, n1, m1


def mlstm_branch(xcm, vm, i_pre, f_pre, o_pre, conv_buf, C0, n0, m0,
                 b_if, conv_w, conv_b, wq_m, wk_m, hnorm_g):
    f32 = jnp.float32
    B, L, _ = xcm.shape
    u, new_buf = causal_conv(xcm, conv_buf, conv_w, conv_b)
    uh = jax.nn.silu(u).reshape(B, L, H_M, DH_M)
    q = jnp.einsum('blhd,hde->bhle', uh, wq_m)
    k = jnp.einsum('blhd,hde->bhle', uh, wk_m) * (DH_M ** -0.5)
    v = vm.reshape(B, L, H_M, DH_M).transpose(0, 2, 1, 3)
    ig = (i_pre + b_if[:H_M]).astype(f32).transpose(0, 2, 1)
    lf = jax.nn.log_sigmoid((f_pre + b_if[H_M:]).astype(f32)).transpose(0, 2, 1)
    h, C1, n1, m1 = mlstm_seq(q, k, v, ig, lf, C0.astype(f32), n0.astype(f32), m0.astype(f32))
    h = rmsnorm(h.transpose(0, 2, 1, 3), hnorm_g).astype(xcm.dtype)
    h = h.reshape(B, L, D_MLSTM) * jax.nn.sigmoid(o_pre)
    return h, new_buf, C1, n1, m1


def mla_attend(qn, qr, q_pos, kn, kr, v, k_pos):
    s = (jnp.einsum('bqhd,bkhd->bhqk', qn, kn) + jnp.einsum('bqhd,bkd->bhqk', qr, kr)).astype(jnp.float32) * ATT_SCALE
    mask = (k_pos[None, :] // CHUNK) <= (q_pos[:, None] // CHUNK)
    s = jnp.where(mask[None, None], s, NEG)
    p = jax.nn.softmax(s, axis=-1).astype(v.dtype)
    return jnp.einsum('bhqk,bkhd->bqhd', p, v)


def mla_branch(c_q, c_kv, k_r, past_ckv, past_kr, qn_g, w_uq, kvn_g, w_ukv, g_qn, g_qr, g_kn, g_kr):
    B, L, _ = c_q.shape
    P = past_ckv.shape[1]
    q_pos = P + jnp.arange(L)
    k_pos = jnp.arange(P + L)
    q = (rmsnorm(c_q, qn_g) @ w_uq).reshape(B, L, H_A, NOPE + ROPE)
    q_nope = rmsnorm(q[..., :NOPE], g_qn)
    q_rope = rope(rmsnorm(q[..., NOPE:], g_qr), q_pos)
    ckv_new = rmsnorm(c_kv, kvn_g)
    kr_new = rope(rmsnorm(k_r, g_kr), q_pos)
    ckv = jnp.concatenate([past_ckv.astype(ckv_new.dtype), ckv_new], axis=1)
    kr = jnp.concatenate([past_kr.astype(kr_new.dtype), kr_new], axis=1)
    kv = (ckv @ w_ukv).reshape(B, P + L, H_A, NOPE + V_DIM)
    k_nope = rmsnorm(kv[..., :NOPE], g_kn)
    v = kv[..., NOPE:]
    if L > Q_BLOCK and L % Q_BLOCK == 0:
        nb = L // Q_BLOCK
        qn_b = q_nope.reshape(B, nb, Q_BLOCK, H_A, NOPE).transpose(1, 0, 2, 3, 4)
        qr_b = q_rope.reshape(B, nb, Q_BLOCK, H_A, ROPE).transpose(1, 0, 2, 3, 4)
        pos_b = q_pos.reshape(nb, Q_BLOCK)
        ob = lax.map(lambda a: mla_attend(a[0], a[1], a[2], k_nope, kr, v, k_pos), (qn_b, qr_b, pos_b))
        o = ob.transpose(1, 0, 2, 3, 4).reshape(B, L, H_A, V_DIM)
    else:
        o = mla_attend(q_nope, q_rope, q_pos, k_nope, kr, v, k_pos)
    return o.reshape(B, L, D_ATT), ckv_new, kr_new


def hybrid_layer(x, past_ckv, past_kr, conv_buf, C0, n0, m0,
                 norm_g, w_in, b_if, conv_w, conv_b, wq_m, wk_m, hnorm_g,
                 qn_g, w_uq, kvn_g, w_ukv, g_qn, g_qr, g_kn, g_kr, w_pm, w_pa, w_out):
    h = rmsnorm(x, norm_g)
    (xcm, vm, i_pre, f_pre, o_pre, z_m, c_q, c_kv, k_r, z_a, g_m, g_a) = split_cols(h @ w_in)
    hm, conv_new, C1, n1, m1 = mlstm_branch(xcm, vm, i_pre, f_pre, o_pre, conv_buf, C0, n0, m0,
                                            b_if, conv_w, conv_b, wq_m, wk_m, hnorm_g)
    ha, ckv_new, kr_new = mla_branch(c_q, c_kv, k_r, past_ckv, past_kr,
                                     qn_g, w_uq, kvn_g, w_ukv, g_qn, g_qr, g_kn, g_kr)
    p_m = (hm * jax.nn.silu(z_m)) @ w_pm
    p_a = (ha * jax.nn.silu(z_a)) @ w_pa
    merged = jax.nn.sigmoid(g_m) * p_m + jax.nn.sigmoid(g_a) * p_a
    y = x + merged @ w_out
    return y, ckv_new, kr_new, conv_new, C1, n1, m1


def setup_inputs(seed: int = 0) -> dict:
    key = jax.random.key(seed)
    ks = jax.random.split(key, 32)
    f32 = jnp.float32

    def nrm(k, shape, scale):
        return jax.random.normal(k, shape, f32) * scale

    def gain(k, shape):
        return 1.0 + 0.02 * jax.random.normal(k, shape, f32)

    b_i = nrm(ks[0], (DEPTH, H_M), 0.1)
    b_f = jnp.linspace(3.0, 6.0, H_M, dtype=f32)[None, :] + nrm(ks[1], (DEPTH, H_M), 0.1)
    return {
        "x_prompt": nrm(ks[2], (BATCH, SEQ, D_MODEL), 1.0),
        "x_sample": nrm(ks[3], (DEC_BATCH, DEC_SEQ, D_MODEL), 1.0),
        "cache_ckv": nrm(ks[4], (DEPTH, DEC_BATCH, PAST_LEN, KV_LORA), 1.0),
        "cache_kr": nrm(ks[5], (DEPTH, DEC_BATCH, PAST_LEN, ROPE), 1.0),
        "state_conv": nrm(ks[6], (DEPTH, DEC_BATCH, CONV_W - 1, D_MLSTM), 1.0),
        "state_C": nrm(ks[7], (DEPTH, DEC_BATCH, H_M, DH_M, DH_M), 0.5),
        "state_n": nrm(ks[8], (DEPTH, DEC_BATCH, H_M, DH_M), 0.5),
        "state_m": nrm(ks[9], (DEPTH, DEC_BATCH, H_M), 1.0),
        "norm_g": gain(ks[10], (DEPTH, D_MODEL)),
        "w_in": nrm(ks[11], (DEPTH, D_MODEL, N_IN), D_MODEL ** -0.5),
        "b_if": jnp.concatenate([b_i, b_f], axis=-1),
        "conv_w": nrm(ks[12], (DEPTH, CONV_W, D_MLSTM), CONV_W ** -0.5),
        "conv_b": nrm(ks[13], (DEPTH, D_MLSTM), 0.02),
        "wq_m": nrm(ks[14], (DEPTH, H_M, DH_M, DH_M), DH_M ** -0.5),
        "wk_m": nrm(ks[15], (DEPTH, H_M, DH_M, DH_M), DH_M ** -0.5),
        "hnorm_g": gain(ks[16], (DEPTH, H_M, DH_M)),
        "qn_g": gain(ks[17], (DEPTH, Q_LORA)),
        "w_uq": nrm(ks[18], (DEPTH, Q_LORA, H_A * (NOPE + ROPE)), Q_LORA ** -0.5),
        "kvn_g": gain(ks[19], (DEPTH, KV_LORA)),
        "w_ukv": nrm(ks[20], (DEPTH, KV_LORA, H_A * (NOPE + V_DIM)), KV_LORA ** -0.5),
        "g_qn": gain(ks[21], (DEPTH, NOPE)),
        "g_qr": gain(ks[22], (DEPTH, ROPE)),
        "g_kn": gain(ks[23], (DEPTH, NOPE)),
        "g_kr": gain(ks[24], (DEPTH, ROPE)),
        "w_pm": nrm(ks[25], (DEPTH, D_MLSTM, D_MODEL), D_MLSTM ** -0.5),
        "w_pa": nrm(ks[26], (DEPTH, D_ATT, D_MODEL), D_ATT ** -0.5),
        "w_out": nrm(ks[27], (DEPTH, D_MODEL, D_MODEL), D_MODEL ** -0.5),
    }


def reference(x_prompt, x_sample, cache_ckv, cache_kr, state_conv, state_C, state_n, state_m,
              norm_g, w_in, b_if, conv_w, conv_b, wq_m, wk_m, hnorm_g,
              qn_g, w_uq, kvn_g, w_ukv, g_qn, g_qr, g_kn, g_kr, w_pm, w_pa, w_out):
    f32 = jnp.float32
    dt = x_prompt.dtype
    B = x_prompt.shape[0]
    yp, ys = x_prompt, x_sample
    outs_p = [[] for _ in range(6)]
    outs_s = [[] for _ in range(6)]
    for l in range(DEPTH):
        pw = (norm_g[l], w_in[l], b_if[l], conv_w[l], conv_b[l], wq_m[l], wk_m[l], hnorm_g[l],
              qn_g[l], w_uq[l], kvn_g[l], w_ukv[l], g_qn[l], g_qr[l], g_kn[l], g_kr[l],
              w_pm[l], w_pa[l], w_out[l])
        yp, ckv_a, kr_a, conv_a, C_a, n_a, m_a = hybrid_layer(
            yp, jnp.zeros((B, 0, KV_LORA), dt), jnp.zeros((B, 0, ROPE), dt),
            jnp.zeros((B, CONV_W - 1, D_MLSTM), dt), jnp.zeros((B, H_M, DH_M, DH_M), f32),
            jnp.zeros((B, H_M, DH_M), f32), jnp.zeros((B, H_M), f32), *pw)
        ys, ckv_b, kr_b, conv_b_, C_b, n_b, m_b = hybrid_layer(
            ys, cache_ckv[l], cache_kr[l], state_conv[l], state_C[l], state_n[l], state_m[l], *pw)
        for lst, a in zip(outs_p, (ckv_a, kr_a, conv_a, C_a, n_a, m_a)):
            lst.append(a.astype(dt))
        for lst, a in zip(outs_s, (ckv_b, kr_b, conv_b_, C_b, n_b, m_b)):
            lst.append(a.astype(dt))
    ckv_p, kr_p, conv_p, C_p, n_p, m_p = [jnp.stack(a) for a in outs_p]
    ckv_s, kr_s, conv_s, C_s, n_s, m_s = [jnp.stack(a) for a in outs_s]
    return (yp, ys, ckv_p, kr_p, conv_p, C_p, n_p, m_p, ckv_s, kr_s, conv_s, C_s, n_s, m_s)
```

```python
import functools
import math

import jax
import jax.numpy as jnp
from jax import lax
from jax.experimental import pallas as pl
from jax.experimental.pallas import tpu as pltpu

F32 = jnp.float32
BF16 = jnp.bfloat16

D_MODEL = 1024
CHUNK = 64
EPS = 1e-6
NEG = -1e30
D_MLSTM = D_MODEL
H_M = 4
DH_M = D_MLSTM // H_M
CONV_W = 4
H_A = 8
NOPE = 128
ROPE = 64
V_DIM = 128
Q_LORA = 384
KV_LORA = 256
D_ATT = H_A * V_DIM
ROPE_THETA = 10000.0
ATT_SCALE = (NOPE + ROPE) ** -0.5
LOG2E = math.log2(math.e)
QK_DIM = NOPE + ROPE
SPLITS = (D_MLSTM, D_MLSTM, H_M, H_M, D_MLSTM, D_MLSTM, Q_LORA, KV_LORA, ROPE, D_ATT, D_MODEL, D_MODEL)

LANES = 128
SUBLANES = 8
VMEM_LIMIT_BYTES = 56 * 1024 * 1024


def _const_spec(shape):
    nd = len(shape)
    return pl.BlockSpec(shape, lambda *_: (0,) * nd, pipeline_mode=pl.Buffered(1))


def _dot(a, b):
    return jnp.dot(a, b, preferred_element_type=F32)


def _dot_nt(a, b):
    return lax.dot_general(a, b, (((1,), (1,)), ((), ())), preferred_element_type=F32)


def _dot_tn(a, b):
    return lax.dot_general(a, b, (((0,), (0,)), ((), ())), preferred_element_type=F32)


def _rms_scale(x, width):
    return lax.rsqrt(jnp.sum(x * x, axis=-1, keepdims=True) * (1.0 / width) + EPS)


def _sigmoid(x):
    return 1.0 / (1.0 + jnp.exp(-x))


def _split3(x):
    a = x.astype(BF16)
    r = x - a.astype(F32)
    b = r.astype(BF16)
    c = (r - b.astype(F32)).astype(BF16)
    return a, b, c


def _rope_half_norm(t, gain, gain_rolled, cos_t, sin_t):
    lane = lax.broadcasted_iota(jnp.int32, t.shape, 1)
    sq = jnp.where(lane < ROPE, t * t, 0.0)
    r = lax.rsqrt(jnp.sum(sq, axis=-1, keepdims=True) * (1.0 / ROPE) + EPS)
    t_sw = pltpu.roll(t, ROPE, axis=1)
    return r * (t * gain * cos_t + t_sw * gain_rolled * sin_t)


def _inproj_kernel(x_ref, ng_ref, wbig_ref, wcq_ref, wsm_ref, qng_ref, wuq_ref, kvng_ref,
                   gkr_ref, gkr2_ref, gqn_ref, gqr_ref, gqr2_ref, cos_ref, sin_ref,
                   xcm_ref, vm_ref, og_ref, sza_ref, sgm_ref, sga_ref, gates_ref,
                   ckv_ref, kr_ref, q_ref, tail_ref):
    x = x_ref[...]
    h = ((x * _rms_scale(x, D_MODEL)) * ng_ref[...]).astype(BF16)
    cos_t = cos_ref[...]
    sin_t = sin_ref[...]

    def seg(i):
        return _dot(h, wbig_ref[:, i * D_MODEL:(i + 1) * D_MODEL])

    xcm = seg(0)
    xcm_ref[...] = xcm.astype(BF16)
    tail_ref[0] = xcm[xcm.shape[0] - SUBLANES:, :]
    vm_ref[...] = seg(1).astype(BF16)
    z = seg(3)
    og_ref[...] = (_sigmoid(seg(2)) * (z * _sigmoid(z))).astype(BF16)
    z = seg(4)
    sza_ref[...] = (z * _sigmoid(z)).astype(BF16)
    sgm_ref[...] = _sigmoid(seg(5)).astype(BF16)
    sga_ref[...] = _sigmoid(seg(6)).astype(BF16)

    t = _dot(h, wsm_ref[...])
    c = t[:, :KV_LORA]
    ckv_ref[...] = (c * _rms_scale(c, KV_LORA)) * kvng_ref[...]
    kr = _rope_half_norm(t[:, KV_LORA:KV_LORA + LANES], gkr_ref[...], gkr2_ref[...], cos_t, sin_t)
    kr_ref[...] = kr[:, :ROPE]
    gates_ref[...] = t[:, KV_LORA + LANES:]

    cq = _dot(h, wcq_ref[...])
    cqn = ((cq * _rms_scale(cq, Q_LORA)) * qng_ref[...]).astype(BF16)
    qscale = ATT_SCALE * LOG2E
    for hd in range(H_A):
        qh = _dot(cqn, wuq_ref[:, hd * 2 * LANES:(hd + 1) * 2 * LANES])
        qn = qh[:, :NOPE]
        qn = (qn * _rms_scale(qn, NOPE)) * gqn_ref[...]
        q_ref[0, hd, :, :NOPE] = (qn * qscale).astype(BF16)
        qr = _rope_half_norm(qh[:, NOPE:], gqr_ref[...], gqr2_ref[...], cos_t, sin_t)
        q_ref[0, hd, :, NOPE:] = (qr[:, :ROPE] * qscale).astype(BF16)


def _inproj(x, prm, cos_t, sin_t, tm):
    B, L, _ = x.shape
    T = B * L
    tps = L // tm
    x2 = x.reshape(T, D_MODEL)
    row = lambda w: pl.BlockSpec((tm, w), lambda i: (i, 0))
    pos = pl.BlockSpec((tm, LANES), lambda i: (i % tps, 0))
    consts = [prm["norm_g"], prm["w_big"], prm["w_cq"], prm["w_small"], prm["qn_g"], prm["w_uq"],
              prm["kvn_g"], prm["g_kr"], prm["g_kr2"], prm["g_qn"], prm["g_qr"], prm["g_qr2"]]
    wide = jax.ShapeDtypeStruct((T, D_MODEL), BF16)
    out_shape = [wide] * 6 + [
        jax.ShapeDtypeStruct((T, LANES), F32),
        jax.ShapeDtypeStruct((T, KV_LORA), F32),
        jax.ShapeDtypeStruct((T, ROPE), F32),
        jax.ShapeDtypeStruct((B, H_A, L, QK_DIM), BF16),
        jax.ShapeDtypeStruct((B, SUBLANES, D_MLSTM), F32),
    ]
    out_specs = [row(D_MODEL)] * 6 + [
        row(LANES), row(KV_LORA), row(ROPE),
        pl.BlockSpec((1, H_A, tm, QK_DIM), lambda i: (i // tps, 0, i % tps, 0)),
        pl.BlockSpec((1, SUBLANES, D_MLSTM), lambda i: (i // tps, 0, 0)),
    ]
    return pl.pallas_call(
        _inproj_kernel,
        grid=(T // tm,),
        in_specs=[row(D_MODEL)] + [_const_spec(c.shape) for c in consts] + [pos, pos],
        out_specs=out_specs,
        out_shape=out_shape,
        compiler_params=pltpu.CompilerParams(
            dimension_semantics=("arbitrary",), vmem_limit_bytes=VMEM_LIMIT_BYTES),
        name="inproj",
    )(x2, *consts, cos_t, sin_t)


def _kvup_kernel(ckv_ref, kr_ref, w_ref, gkn_ref, k_ref, v_ref):
    c = ckv_ref[0].astype(BF16)
    kr = kr_ref[0].astype(BF16)
    for hd in range(H_A):
        kv = _dot(c, w_ref[:, hd * 2 * LANES:(hd + 1) * 2 * LANES])
        kn = kv[:, :NOPE]
        k_ref[0, hd, :, :NOPE] = ((kn * _rms_scale(kn, NOPE)) * gkn_ref[...]).astype(BF16)
        k_ref[0, hd, :, NOPE:] = kr
        v_ref[0, hd] = kv[:, NOPE:].astype(BF16)


def _kvup(ckv, kr, prm, tm):
    B, T, _ = ckv.shape
    return pl.pallas_call(
        _kvup_kernel,
        grid=(B, T // tm),
        in_specs=[pl.BlockSpec((1, tm, KV_LORA), lambda b, i: (b, i, 0)),
                  pl.BlockSpec((1, tm, ROPE), lambda b, i: (b, i, 0)),
                  _const_spec(prm["w_ukv"].shape), _const_spec(prm["g_kn"].shape)],
        out_specs=[pl.BlockSpec((1, H_A, tm, QK_DIM), lambda b, i: (b, 0, i, 0)),
                   pl.BlockSpec((1, H_A, tm, V_DIM), lambda b, i: (b, 0, i, 0))],
        out_shape=[jax.ShapeDtypeStruct((B, H_A, T, QK_DIM), BF16),
                   jax.ShapeDtypeStruct((B, H_A, T, V_DIM), BF16)],
        compiler_params=pltpu.CompilerParams(
            dimension_semantics=("arbitrary", "arbitrary"), vmem_limit_bytes=VMEM_LIMIT_BYTES),
        name="kvup",
    )(ckv, kr, prm["w_ukv"], prm["g_kn"])


def _attn_kernel(q_ref, k_ref, v_ref, o_ref, *, tq, tk, past):
    qi = pl.program_id(2)
    q = q_ref[0, 0]

    def update(carry, k, v, mask):
        m, l, acc = carry
        s = _dot_nt(q, k)
        if mask is not None:
            s = jnp.where(mask, s, NEG)
        m_new = jnp.maximum(m, jnp.max(s, axis=1, keepdims=True))
        alpha = jnp.exp2(m - m_new)
        p = jnp.exp2(s - m_new)
        l = alpha * l + jnp.sum(p, axis=1, keepdims=True)
        acc = alpha * acc + _dot(p.astype(BF16), v)
        return m_new, l, acc

    def body(j, carry):
        ks = pl.multiple_of(j * tk, tk)
        return update(carry, k_ref[0, 0, pl.ds(ks, tk), :], v_ref[0, 0, pl.ds(ks, tk), :], None)

    diag0 = past + qi * tq
    carry = (jnp.full((tq, 1), NEG, F32), jnp.zeros((tq, 1), F32), jnp.zeros((tq, V_DIM), F32))
    carry = lax.fori_loop(0, diag0 // tk, body, carry)
    ks = pl.multiple_of(diag0, tq)
    rows = lax.broadcasted_iota(jnp.int32, (tq, tq), 0) // CHUNK
    cols = lax.broadcasted_iota(jnp.int32, (tq, tq), 1) // CHUNK
    _, l, acc = update(carry, k_ref[0, 0, pl.ds(ks, tq), :], v_ref[0, 0, pl.ds(ks, tq), :],
                       cols <= rows)
    o_ref[0] = (acc / l).astype(BF16)


def _attention(q, k, v, tq, tk):
    B, H, L, _ = q.shape
    T = k.shape[2]
    past = T - L
    assert past % tk == 0 and (tq % tk == 0 or L == tq)
    return pl.pallas_call(
        functools.partial(_attn_kernel, tq=tq, tk=tk, past=past),
        grid=(B, H, L // tq),
        in_specs=[pl.BlockSpec((1, 1, tq, QK_DIM), lambda b, h, i: (b, h, i, 0)),
                  pl.BlockSpec((1, 1, T, QK_DIM), lambda b, h, i: (b, h, 0, 0)),
                  pl.BlockSpec((1, 1, T, V_DIM), lambda b, h, i: (b, h, 0, 0))],
        out_specs=pl.BlockSpec((1, tq, V_DIM), lambda b, h, i: (b, i, h)),
        out_shape=jax.ShapeDtypeStruct((B, L, D_ATT), BF16),
        compiler_params=pltpu.CompilerParams(
            dimension_semantics=("arbitrary", "arbitrary", "arbitrary"),
            vmem_limit_bytes=VMEM_LIMIT_BYTES),
        name="attention",
    )(q, k, v)


def _mlstm_kernel(xcm_ref, vm_ref, gates_ref, og_ref, sgm_ref, cbuf_ref, c0_ref, n0_ref, m0_ref,
                  bif_ref, cw_ref, cb_ref, wq_ref, wk_ref, hg_ref, wpm_ref,
                  mm_ref, c_ref, n_ref, m_ref, xpad_ref, hn_ref, *, ch):
    ci = pl.program_id(1)

    @pl.when(ci == 0)
    def _():
        c_ref[...] = c0_ref[...]
        n_ref[...] = n0_ref[...]
        m_ref[...] = m0_ref[...]
        xpad_ref[0:SUBLANES, :] = cbuf_ref[0]

    xpad_ref[SUBLANES:, :] = xcm_ref[0].astype(F32)
    u = cb_ref[...]
    for j in range(CONV_W):
        off = SUBLANES - (CONV_W - 1) + j
        u = u + xpad_ref[off:off + ch, :] * cw_ref[j:j + 1, :]
    xpad_ref[0:SUBLANES, :] = xpad_ref[ch:ch + SUBLANES, :]
    su = (u * _sigmoid(u)).astype(BF16)

    gpre = gates_ref[0] + bif_ref[...]
    lane = lax.broadcasted_iota(jnp.int32, (ch, LANES), 1)
    lf = jnp.minimum(gpre, 0.0) - jnp.log1p(jnp.exp(-jnp.abs(gpre)))
    lf = jnp.where((lane >= H_M) & (lane < 2 * H_M), lf, 0.0)
    tri = (lax.broadcasted_iota(jnp.int32, (ch, ch), 1)
           <= lax.broadcasted_iota(jnp.int32, (ch, ch), 0))
    tri_b = tri.astype(BF16)
    bcum = sum(_dot(tri_b, part) for part in _split3(lf))
    xparts = _split3(jnp.where(lane < H_M, gpre, -bcum))

    for hd in range(H_M):
        sl = slice(hd * DH_M, (hd + 1) * DH_M)
        m0 = m_ref[0, hd:hd + 1, 0:1]
        bc = bcum[:, H_M + hd:H_M + hd + 1]
        igc = gpre[:, hd:hd + 1]
        pick = ((lane == hd) | (lane == H_M + hd)).astype(BF16)
        drow = sum(_dot_nt(pick, part) for part in xparts)
        dlog = jnp.where(tri, bc + drow, NEG)
        g = bc + m0
        m = jnp.maximum(g, jnp.max(dlog, axis=1, keepdims=True))
        w_intra = jnp.exp(dlog - m)
        w_inter = jnp.exp(g - m)

        su_h = su[:, sl]
        qh = _dot(su_h, wq_ref[hd])
        kh = _dot(su_h, wk_ref[hd])
        qb = qh.astype(BF16)
        v = vm_ref[0, :, sl]
        c0 = c_ref[0, hd]
        n0 = n_ref[0, hd:hd + 1, :]
        s = _dot_nt(qb, kh.astype(BF16)) * w_intra
        num = w_inter * _dot(qb, c0.astype(BF16)) + _dot(s.astype(BF16), v)
        den = (w_inter * jnp.sum(qh * n0, axis=1, keepdims=True)
               + jnp.sum(s, axis=1, keepdims=True))
        hh = num / jnp.maximum(jnp.abs(den), jnp.exp(-m))
        hn_ref[:, sl] = (hh * _rms_scale(hh, DH_M)) * hg_ref[:, sl]

        m_last = m[ch - 1:ch, :]
        b_last = bc[ch - 1:ch, :]
        w_state = jnp.exp(b_last - bc + igc - m_last)
        decay = jnp.exp(b_last + m0 - m_last)
        kw = kh * w_state
        c_ref[0, hd] = decay * c0 + _dot_tn(kw.astype(BF16), v)
        n_ref[0, hd:hd + 1, :] = decay * n0 + jnp.sum(kw, axis=0, keepdims=True)
        m_ref[0, hd:hd + 1, :] = jnp.broadcast_to(m_last, (1, LANES))

    hm = (hn_ref[...] * og_ref[0].astype(F32)).astype(BF16)
    pm = _dot(hm, wpm_ref[...])
    mm_ref[0] = (sgm_ref[0].astype(F32) * pm).astype(BF16)


def _mlstm(xcm, vm, gates, og, sgm, cbuf, c0, n0, m0b, prm, ch):
    B, L, _ = xcm.shape
    seq = lambda w: pl.BlockSpec((1, ch, w), lambda b, c: (b, c, 0))
    per_b = lambda *s: pl.BlockSpec((1,) + s, lambda b, c: (b,) + (0,) * len(s))
    consts = [prm["b_if"], prm["conv_w"], prm["conv_b"], prm["wq_m"], prm["wk_m"], prm["hnorm_g"],
              prm["w_pm"]]
    return pl.pallas_call(
        functools.partial(_mlstm_kernel, ch=ch),
        grid=(B, L // ch),
        in_specs=[seq(D_MLSTM), seq(D_MLSTM), seq(LANES), seq(D_MLSTM), seq(D_MODEL),
                  per_b(SUBLANES, D_MLSTM), per_b(H_M, DH_M, DH_M), per_b(H_M, DH_M),
                  per_b(H_M, LANES)] + [_const_spec(c.shape) for c in consts],
        out_specs=[seq(D_MODEL), per_b(H_M, DH_M, DH_M), per_b(H_M, DH_M), per_b(H_M, LANES)],
        out_shape=[jax.ShapeDtypeStruct((B, L, D_MODEL), BF16),
                   jax.ShapeDtypeStruct((B, H_M, DH_M, DH_M), F32),
                   jax.ShapeDtypeStruct((B, H_M, DH_M), F32),
                   jax.ShapeDtypeStruct((B, H_M, LANES), F32)],
        scratch_shapes=[pltpu.VMEM((ch + SUBLANES, D_MLSTM), F32),
                        pltpu.VMEM((ch, D_MLSTM), F32)],
        compiler_params=pltpu.CompilerParams(
            dimension_semantics=("arbitrary", "arbitrary"), vmem_limit_bytes=VMEM_LIMIT_BYTES),
        name="mlstm",
    )(xcm, vm, gates, og, sgm, cbuf, c0, n0, m0b, *consts)


def _outproj_kernel(o_ref, sza_ref, sga_ref, mm_ref, x_ref, wpa_ref, wout_ref, y_ref):
    a = (o_ref[...].astype(F32) * sza_ref[...].astype(F32)).astype(BF16)
    pa = _dot(a, wpa_ref[...])
    merged = mm_ref[...].astype(F32) + sga_ref[...].astype(F32) * pa
    y_ref[...] = x_ref[...] + _dot(merged.astype(BF16), wout_ref[...])


def _outproj(o, sza, sga, mm, x2, prm, tm):
    T = x2.shape[0]
    row = pl.BlockSpec((tm, D_MODEL), lambda i: (i, 0))
    return pl.pallas_call(
        _outproj_kernel,
        grid=(T // tm,),
        in_specs=[row] * 5 + [_const_spec(prm["w_pa"].shape), _const_spec(prm["w_out"].shape)],
        out_specs=row,
        out_shape=jax.ShapeDtypeStruct((T, D_MODEL), F32),
        compiler_params=pltpu.CompilerParams(
            dimension_semantics=("arbitrary",), vmem_limit_bytes=VMEM_LIMIT_BYTES),
        name="outproj",
    )(o, sza, sga, mm, x2, prm["w_pa"], prm["w_out"])


def _swap_halves(a):
    half = ROPE // 2
    return jnp.concatenate([a[..., half:], a[..., :half]], axis=-1)


def _prepare_params(norm_g, w_in, b_if, conv_w, conv_b, wq_m, wk_m, hnorm_g, qn_g, w_uq, kvn_g,
                    w_ukv, g_qn, g_qr, g_kn, g_kr, w_pm, w_pa, w_out):
    cols, start = [], 0
    for w in SPLITS:
        cols.append(w_in[:, start:start + w])
        start += w
    xcm, vm, ig, fg, og, zm, cq, ckv, kr, za, gm, ga = cols
    pad = jnp.zeros((D_MODEL, LANES - 2 * H_M), F32)
    w_small = jnp.concatenate([ckv, kr, _swap_halves(kr), ig, fg, pad], axis=1)
    wuq = w_uq.reshape(Q_LORA, H_A, QK_DIM)
    wuq = jnp.concatenate([wuq, _swap_halves(wuq[..., NOPE:])], axis=-1)
    row = lambda a: a.reshape(1, -1).astype(F32)
    pair = lambda g: row(jnp.concatenate([g, _swap_halves(g)]))
    pair2 = lambda g: row(jnp.concatenate([_swap_halves(g), g]))
    return {
        "norm_g": row(norm_g),
        "w_big": jnp.concatenate([xcm, vm, og, zm, za, gm, ga], axis=1).astype(BF16),
        "w_cq": cq.astype(BF16),
        "w_small": w_small.astype(BF16),
        "qn_g": row(qn_g),
        "w_uq": wuq.reshape(Q_LORA, H_A * 2 * LANES).astype(BF16),
        "kvn_g": row(kvn_g),
        "g_kr": pair(g_kr), "g_kr2": pair2(g_kr),
        "g_qn": row(g_qn),
        "g_qr": pair(g_qr), "g_qr2": pair2(g_qr),
        "w_ukv": w_ukv.astype(BF16),
        "g_kn": row(g_kn),
        "b_if": row(jnp.concatenate([b_if, jnp.zeros((LANES - 2 * H_M,), F32)])),
        "conv_w": conv_w, "conv_b": row(conv_b),
        "wq_m": wq_m.astype(BF16),
        "wk_m": (wk_m * (DH_M ** -0.5)).astype(BF16),
        "hnorm_g": row(hnorm_g),
        "w_pm": w_pm.astype(BF16), "w_pa": w_pa.astype(BF16), "w_out": w_out.astype(BF16),
    }


def _rope_tables(past, length):
    half = ROPE // 2
    inv = jnp.power(ROPE_THETA, -jnp.arange(half, dtype=F32) / half)
    ang = (past + jnp.arange(length)).astype(F32)[:, None] * inv[None, :]
    cos, sin = jnp.cos(ang), jnp.sin(ang)
    return jnp.tile(cos, (1, 4)), jnp.tile(jnp.concatenate([-sin, sin], axis=1), (1, 2))


def _hybrid_layer(x, past_ckv, past_kr, conv_buf, c0, n0, m0, prm, *, tm, tq, tk, tkv, ch):
    B, L, _ = x.shape
    past = past_ckv.shape[1]
    cos_t, sin_t = _rope_tables(past, L)
    (xcm, vm, og, sza, sgm, sga, gates, ckv_new, kr_new, q, tail) = _inproj(x, prm, cos_t, sin_t, tm)
    seq = lambda a: a.reshape(B, L, a.shape[-1])
    ckv_new, kr_new = seq(ckv_new), seq(kr_new)

    ckv_all = jnp.concatenate([past_ckv, ckv_new], axis=1) if past else ckv_new
    kr_all = jnp.concatenate([past_kr, kr_new], axis=1) if past else kr_new
    k, v = _kvup(ckv_all, kr_all, prm, tkv)
    o = _attention(q, k, v, tq, tk)

    cbuf = jnp.concatenate(
        [jnp.zeros((B, SUBLANES - (CONV_W - 1), D_MLSTM), F32), conv_buf.astype(F32)], axis=1)
    m0b = jnp.broadcast_to(m0.astype(F32)[:, :, None], (B, H_M, LANES))
    mm, c1, n1, m1 = _mlstm(seq(xcm), seq(vm), seq(gates), seq(og), seq(sgm), cbuf,
                            c0.astype(F32), n0.astype(F32), m0b, prm, ch)

    y = _outproj(o.reshape(B * L, D_ATT), sza, sga, mm.reshape(B * L, D_MODEL),
                 x.reshape(B * L, D_MODEL), prm, tm)
    conv_new = tail[:, SUBLANES - (CONV_W - 1):, :]
    return y.reshape(B, L, D_MODEL), ckv_new, kr_new, conv_new, c1, n1, m1[:, :, 0]


def kernel(x_prompt, x_sample, cache_ckv, cache_kr, state_conv, state_C, state_n, state_m,
           norm_g, w_in, b_if, conv_w, conv_b, wq_m, wk_m, hnorm_g,
           qn_g, w_uq, kvn_g, w_ukv, g_qn, g_qr, g_kn, g_kr, w_pm, w_pa, w_out):
    dt = x_prompt.dtype
    B = x_prompt.shape[0]
    depth = norm_g.shape[0]
    yp, ys = x_prompt, x_sample
    outs_p = [[] for _ in range(6)]
    outs_s = [[] for _ in range(6)]
    for l in range(depth):
        prm = _prepare_params(norm_g[l], w_in[l], b_if[l], conv_w[l], conv_b[l], wq_m[l], wk_m[l],
                              hnorm_g[l], qn_g[l], w_uq[l], kvn_g[l], w_ukv[l], g_qn[l], g_qr[l],
                              g_kn[l], g_kr[l], w_pm[l], w_pa[l], w_out[l])
        yp, *res_p = _hybrid_layer(
            yp, jnp.zeros((B, 0, KV_LORA), dt), jnp.zeros((B, 0, ROPE), dt),
            jnp.zeros((B, CONV_W - 1, D_MLSTM), dt), jnp.zeros((B, H_M, DH_M, DH_M), F32),
            jnp.zeros((B, H_M, DH_M), F32), jnp.zeros((B, H_M), F32), prm,
            tm=256, tq=256, tk=256, tkv=512, ch=256)
        Ls = x_sample.shape[1]
        ys, *res_s = _hybrid_layer(
            ys, cache_ckv[l], cache_kr[l], state_conv[l], state_C[l], state_n[l], state_m[l], prm,
            tm=Ls, tq=Ls, tk=256, tkv=cache_ckv.shape[2] + Ls, ch=Ls)
        for lst, a in zip(outs_p, res_p):
            lst.append(a.astype(dt))
        for lst, a in zip(outs_s, res_s):
            lst.append(a.astype(dt))
    return (yp, ys) + tuple(jnp.stack(a) for a in outs_p) + tuple(jnp.stack(a) for a in outs_s)
```

```python
import functools
import math

import jax
import jax.numpy as jnp
from jax import lax
from jax.experimental import pallas as pl
from jax.experimental.pallas import tpu as pltpu

F32 = jnp.float32
BF16 = jnp.bfloat16

D_MODEL = 1024
CHUNK = 64
EPS = 1e-6
NEG = -1e30
D_MLSTM = D_MODEL
H_M = 4
DH_M = D_MLSTM // H_M
CONV_W = 4
H_A = 8
NOPE = 128
ROPE = 64
V_DIM = 128
Q_LORA = 384
KV_LORA = 256
D_ATT = H_A * V_DIM
ROPE_THETA = 10000.0
ATT_SCALE = (NOPE + ROPE) ** -0.5
LOG2E = math.log2(math.e)
QK_DIM = NOPE + ROPE
QK_PAD = 256
V_PAD = 256
SAFE_BOUND = 50.0
SPLITS = (D_MLSTM, D_MLSTM, H_M, H_M, D_MLSTM, D_MLSTM, Q_LORA, KV_LORA, ROPE, D_ATT, D_MODEL, D_MODEL)

LANES = 128
SUBLANES = 8
VMEM_LIMIT_BYTES = 56 * 1024 * 1024


def _const_spec(shape):
    nd = len(shape)
    return pl.BlockSpec(shape, lambda *_: (0,) * nd, pipeline_mode=pl.Buffered(1))


def _dot(a, b):
    return jnp.dot(a, b, preferred_element_type=F32)


def _dot_nt(a, b):
    return lax.dot_general(a, b, (((1,), (1,)), ((), ())), preferred_element_type=F32)


def _dot_tn(a, b):
    return lax.dot_general(a, b, (((0,), (0,)), ((), ())), preferred_element_type=F32)


def _rms_scale(x, width):
    return lax.rsqrt(jnp.sum(x * x, axis=-1, keepdims=True) * (1.0 / width) + EPS)


def _sigmoid(x):
    return 1.0 / (1.0 + jnp.exp(-x))


def _split3(x):
    a = x.astype(BF16)
    r = x - a.astype(F32)
    b = r.astype(BF16)
    c = (r - b.astype(F32)).astype(BF16)
    return a, b, c


def _rope_half_norm(t, gain, gain_rolled, cos_t, sin_t):
    lane = lax.broadcasted_iota(jnp.int32, t.shape, 1)
    sq = jnp.where(lane < ROPE, t * t, 0.0)
    r = lax.rsqrt(jnp.sum(sq, axis=-1, keepdims=True) * (1.0 / ROPE) + EPS)
    t_sw = pltpu.roll(t, ROPE, axis=1)
    return r * (t * gain * cos_t + t_sw * gain_rolled * sin_t)


def _inproj_kernel(x_ref, ng_ref, wbig_ref, wcq_ref, wsm_ref, qng_ref, wuq_ref, kvng_ref,
                   gkr_ref, gkr2_ref, gqn_ref, gqr_ref, gqr2_ref, cos_ref, sin_ref,
                   xcm_ref, vm_ref, og_ref, sza_ref, sgm_ref, sga_ref, gates_ref,
                   ckv_ref, kr_ref, q_ref, tail_ref):
    x = x_ref[...]
    h = ((x * _rms_scale(x, D_MODEL)) * ng_ref[...]).astype(BF16)
    cos_t = cos_ref[...]
    sin_t = sin_ref[...]

    def seg(i):
        return _dot(h, wbig_ref[:, i * D_MODEL:(i + 1) * D_MODEL])

    xcm = seg(0)
    xcm_ref[...] = xcm.astype(BF16)
    tail_ref[0] = xcm[xcm.shape[0] - SUBLANES:, :]
    vm_ref[...] = seg(1).astype(BF16)
    z = seg(3)
    og_ref[...] = (_sigmoid(seg(2)) * (z * _sigmoid(z))).astype(BF16)
    z = seg(4)
    sza_ref[...] = (z * _sigmoid(z)).astype(BF16)
    sgm_ref[...] = _sigmoid(seg(5)).astype(BF16)
    sga_ref[...] = _sigmoid(seg(6)).astype(BF16)

    t = _dot(h, wsm_ref[...])
    c = t[:, :KV_LORA]
    ckv_ref[...] = (c * _rms_scale(c, KV_LORA)) * kvng_ref[...]
    kr = _rope_half_norm(t[:, KV_LORA:KV_LORA + LANES], gkr_ref[...], gkr2_ref[...], cos_t, sin_t)
    kr_ref[...] = kr[:, :ROPE]
    gates_ref[...] = t[:, KV_LORA + LANES:]

    cq = _dot(h, wcq_ref[...])
    cqn = ((cq * _rms_scale(cq, Q_LORA)) * qng_ref[...]).astype(BF16)
    qscale = ATT_SCALE * LOG2E
    for hd in range(H_A):
        qh = _dot(cqn, wuq_ref[:, hd * 2 * LANES:(hd + 1) * 2 * LANES])
        qn = qh[:, :NOPE]
        qn = (qn * _rms_scale(qn, NOPE)) * gqn_ref[...]
        q_ref[0, hd, :, :NOPE] = (qn * qscale).astype(BF16)
        qr = _rope_half_norm(qh[:, NOPE:], gqr_ref[...], gqr2_ref[...], cos_t, sin_t)
        lane = lax.broadcasted_iota(jnp.int32, qr.shape, 1)
        q_ref[0, hd, :, NOPE:] = jnp.where(lane < ROPE, qr * qscale, 0.0).astype(BF16)


def _inproj(x, prm, cos_t, sin_t, tm):
    B, L, _ = x.shape
    T = B * L
    tps = L // tm
    x2 = x.reshape(T, D_MODEL)
    row = lambda w: pl.BlockSpec((tm, w), lambda i: (i, 0))
    pos = pl.BlockSpec((tm, LANES), lambda i: (i % tps, 0))
    consts = [prm["norm_g"], prm["w_big"], prm["w_cq"], prm["w_small"], prm["qn_g"], prm["w_uq"],
              prm["kvn_g"], prm["g_kr"], prm["g_kr2"], prm["g_qn"], prm["g_qr"], prm["g_qr2"]]
    wide = jax.ShapeDtypeStruct((T, D_MODEL), BF16)
    out_shape = [wide] * 6 + [
        jax.ShapeDtypeStruct((T, LANES), F32),
        jax.ShapeDtypeStruct((T, KV_LORA), F32),
        jax.ShapeDtypeStruct((T, ROPE), F32),
        jax.ShapeDtypeStruct((B, H_A, L, QK_PAD), BF16),
        jax.ShapeDtypeStruct((B, SUBLANES, D_MLSTM), F32),
    ]
    out_specs = [row(D_MODEL)] * 6 + [
        row(LANES), row(KV_LORA), row(ROPE),
        pl.BlockSpec((1, H_A, tm, QK_PAD), lambda i: (i // tps, 0, i % tps, 0)),
        pl.BlockSpec((1, SUBLANES, D_MLSTM), lambda i: (i // tps, 0, 0)),
    ]
    return pl.pallas_call(
        _inproj_kernel,
        grid=(T // tm,),
        in_specs=[row(D_MODEL)] + [_const_spec(c.shape) for c in consts] + [pos, pos],
        out_specs=out_specs,
        out_shape=out_shape,
        compiler_params=pltpu.CompilerParams(
            dimension_semantics=("arbitrary",), vmem_limit_bytes=VMEM_LIMIT_BYTES),
        name="inproj",
    )(x2, *consts, cos_t, sin_t)


def _kvup_kernel(ckv_ref, kr_ref, w_ref, gkn_ref, k_ref, v_ref, kmax_ref):
    c = ckv_ref[0].astype(BF16)
    kr = kr_ref[0]
    kr2 = jnp.sum(kr * kr, axis=1, keepdims=True)
    lane = lax.broadcasted_iota(jnp.int32, (c.shape[0], 2 * LANES), 1)
    for hd in range(H_A):
        kv = _dot(c, w_ref[:, hd * 2 * LANES:(hd + 1) * 2 * LANES])
        kn = kv[:, :NOPE]
        kn = (kn * _rms_scale(kn, NOPE)) * gkn_ref[...]
        k_ref[0, hd, :, :NOPE] = kn.astype(BF16)
        k_ref[0, hd, :, NOPE:] = kr.astype(BF16)
        v_aug = jnp.where(lane < V_DIM, pltpu.roll(kv, V_DIM, axis=1),
                          (lane == V_DIM).astype(F32))
        v_ref[0, hd] = v_aug.astype(BF16)
        k2 = jnp.sum(kn * kn, axis=1, keepdims=True) + kr2
        kmax_ref[0, 0, hd:hd + 1, :] = jnp.broadcast_to(
            jnp.max(k2, axis=0, keepdims=True), (1, LANES))


def _kvup(ckv, kr, prm, tm):
    B, T, _ = ckv.shape
    return pl.pallas_call(
        _kvup_kernel,
        grid=(B, T // tm),
        in_specs=[pl.BlockSpec((1, tm, KV_LORA), lambda b, i: (b, i, 0)),
                  pl.BlockSpec((1, tm, LANES), lambda b, i: (b, i, 0)),
                  _const_spec(prm["w_ukv"].shape), _const_spec(prm["g_kn"].shape)],
        out_specs=[pl.BlockSpec((1, H_A, tm, QK_PAD), lambda b, i: (b, 0, i, 0)),
                   pl.BlockSpec((1, H_A, tm, V_PAD), lambda b, i: (b, 0, i, 0)),
                   pl.BlockSpec((1, 1, H_A, LANES), lambda b, i: (b, i, 0, 0))],
        out_shape=[jax.ShapeDtypeStruct((B, H_A, T, QK_PAD), BF16),
                   jax.ShapeDtypeStruct((B, H_A, T, V_PAD), BF16),
                   jax.ShapeDtypeStruct((B, T // tm, H_A, LANES), F32)],
        compiler_params=pltpu.CompilerParams(
            dimension_semantics=("arbitrary", "arbitrary"), vmem_limit_bytes=VMEM_LIMIT_BYTES),
        name="kvup",
    )(ckv, kr, prm["w_ukv"], prm["g_kn"])


HEADS_PER_STEP = 2


def _attn_kernel(safe_ref, q_ref, k_ref, v_ref, negb_ref, o_ref, acc_ref, qa_ref, m_ref,
                 *, tq, past):
    b, hp, qi = pl.program_id(0), pl.program_id(1), pl.program_id(2)
    diag0 = past + qi * tq
    wide = 2 * tq
    heads = range(HEADS_PER_STEP)

    def chunk_mask(width):
        rows = lax.broadcasted_iota(jnp.int32, (tq, width), 0) // CHUNK
        cols = lax.broadcasted_iota(jnp.int32, (tq, width), 1) // CHUNK
        return cols <= rows + (width - tq) // CHUNK

    acc_ref[...] = jnp.zeros_like(acc_ref)

    @pl.when(safe_ref[b, hp] == 1)
    def _bounded():
        lane = lax.broadcasted_iota(jnp.int32, (tq, QK_PAD), 1)
        for h in heads:
            qa_ref[h] = jnp.where(lane == QK_DIM, negb_ref[0, h], q_ref[0, h].astype(F32)
                                  ).astype(BF16)

        def step(ks, size, msk):
            for h in heads:
                s = _dot_nt(qa_ref[h], k_ref[0, h, pl.ds(ks, size), :])
                if msk is not None:
                    s = jnp.where(msk, s, NEG)
                p = jnp.exp2(s).astype(BF16)
                acc_ref[h] += _dot(p, v_ref[0, h, pl.ds(ks, size), :])

        @pl.loop(0, diag0 // wide)
        def _(j):
            step(pl.multiple_of(j * wide, wide), wide, None)

        @pl.when(diag0 % wide == 0)
        def _():
            step(pl.multiple_of(diag0, tq), tq, chunk_mask(tq))

        @pl.when(diag0 % wide != 0)
        def _():
            step(pl.multiple_of(diag0 - tq, tq), wide, chunk_mask(wide))

    @pl.when(safe_ref[b, hp] == 0)
    def _online():
        m_ref[...] = jnp.full_like(m_ref, NEG)

        def step(ks, msk):
            for h in heads:
                s = _dot_nt(q_ref[0, h], k_ref[0, h, pl.ds(ks, tq), :])
                if msk is not None:
                    s = jnp.where(msk, s, NEG)
                m_old = m_ref[h]
                m_new = jnp.maximum(m_old, jnp.max(s, axis=1, keepdims=True))
                p = jnp.exp2(s - m_new).astype(BF16)
                acc_ref[h] = (jnp.exp2(m_old - m_new) * acc_ref[h]
                              + _dot(p, v_ref[0, h, pl.ds(ks, tq), :]))
                m_ref[h] = m_new

        @pl.loop(0, diag0 // tq)
        def _(j):
            step(pl.multiple_of(j * tq, tq), None)

        step(pl.multiple_of(diag0, tq), chunk_mask(tq))

    for h in heads:
        acc = acc_ref[h]
        o_ref[0, :, h * V_DIM:(h + 1) * V_DIM] = (
            acc[:, :V_DIM] / acc[:, V_DIM:V_DIM + 1]).astype(BF16)


def _attention(q, k, v, kmax2, q_bound2, tq):
    B, H, L, _ = q.shape
    T = k.shape[2]
    past = T - L
    hp = HEADS_PER_STEP
    assert past % (2 * tq) == 0 and L % tq == 0 and H % hp == 0
    bound = jnp.sqrt(kmax2 * q_bound2) * 1.01
    safe = jnp.all(bound.reshape(B, H // hp, hp) <= SAFE_BOUND, axis=-1).astype(jnp.int32)
    negb = jnp.broadcast_to(-bound[:, :, None, None], (B, H, 1, QK_PAD))
    grid_spec = pltpu.PrefetchScalarGridSpec(
        num_scalar_prefetch=1,
        grid=(B, H // hp, L // tq),
        in_specs=[pl.BlockSpec((1, hp, tq, QK_PAD), lambda b, h, i, s: (b, h, i, 0)),
                  pl.BlockSpec((1, hp, T, QK_PAD), lambda b, h, i, s: (b, h, 0, 0)),
                  pl.BlockSpec((1, hp, T, V_PAD), lambda b, h, i, s: (b, h, 0, 0)),
                  pl.BlockSpec((1, hp, 1, QK_PAD), lambda b, h, i, s: (b, h, 0, 0))],
        out_specs=pl.BlockSpec((1, tq, hp * V_DIM), lambda b, h, i, s: (b, i, h)),
        scratch_shapes=[pltpu.VMEM((hp, tq, V_PAD), F32), pltpu.VMEM((hp, tq, QK_PAD), BF16),
                        pltpu.VMEM((hp, tq, 1), F32)],
    )
    return pl.pallas_call(
        functools.partial(_attn_kernel, tq=tq, past=past),
        grid_spec=grid_spec,
        out_shape=jax.ShapeDtypeStruct((B, L, D_ATT), BF16),
        compiler_params=pltpu.CompilerParams(
            dimension_semantics=("arbitrary", "arbitrary", "arbitrary"),
            vmem_limit_bytes=VMEM_LIMIT_BYTES),
        name="attention",
    )(safe, q, k, v, negb)


def _mlstm_kernel(xcm_ref, vm_ref, gates_ref, og_ref, sgm_ref, cbuf_ref, c0_ref, n0_ref, m0_ref,
                  bif_ref, cw_ref, cb_ref, wq_ref, wk_ref, hg_ref, wpm_ref,
                  mm_ref, c_ref, n_ref, m_ref, xpad_ref, hn_ref, *, ch):
    ci = pl.program_id(1)

    @pl.when(ci == 0)
    def _():
        c_ref[...] = c0_ref[...]
        n_ref[...] = n0_ref[...]
        m_ref[...] = m0_ref[...]
        xpad_ref[0:SUBLANES, :] = cbuf_ref[0]

    xpad_ref[SUBLANES:, :] = xcm_ref[0].astype(F32)
    u = cb_ref[...]
    for j in range(CONV_W):
        off = SUBLANES - (CONV_W - 1) + j
        u = u + xpad_ref[off:off + ch, :] * cw_ref[j:j + 1, :]
    xpad_ref[0:SUBLANES, :] = xpad_ref[ch:ch + SUBLANES, :]
    su = (u * _sigmoid(u)).astype(BF16)

    gpre = gates_ref[0] + bif_ref[...]
    lane = lax.broadcasted_iota(jnp.int32, (ch, LANES), 1)
    lf = jnp.minimum(gpre, 0.0) - jnp.log1p(jnp.exp(-jnp.abs(gpre)))
    lf = jnp.where((lane >= H_M) & (lane < 2 * H_M), lf, 0.0)
    tri = (lax.broadcasted_iota(jnp.int32, (ch, ch), 1)
           <= lax.broadcasted_iota(jnp.int32, (ch, ch), 0))
    tri_b = tri.astype(BF16)
    bcum = sum(_dot(tri_b, part) for part in _split3(lf))
    xparts = _split3(jnp.where(lane < H_M, gpre, -bcum))

    for hd in range(H_M):
        sl = slice(hd * DH_M, (hd + 1) * DH_M)
        m0 = m_ref[0, hd:hd + 1, 0:1]
        bc = bcum[:, H_M + hd:H_M + hd + 1]
        igc = gpre[:, hd:hd + 1]
        pick = ((lane == hd) | (lane == H_M + hd)).astype(BF16)
        drow = sum(_dot_nt(pick, part) for part in xparts)
        dlog = jnp.where(tri, bc + drow, NEG)
        g = bc + m0
        m = jnp.maximum(g, jnp.max(dlog, axis=1, keepdims=True))
        w_intra = jnp.exp(dlog - m)
        w_inter = jnp.exp(g - m)

        su_h = su[:, sl]
        qh = _dot(su_h, wq_ref[hd])
        kh = _dot(su_h, wk_ref[hd])
        qb = qh.astype(BF16)
        v = vm_ref[0, :, sl]
        c0 = c_ref[0, hd]
        n0 = n_ref[0, hd:hd + 1, :]
        s = _dot_nt(qb, kh.astype(BF16)) * w_intra
        num = w_inter * _dot(qb, c0.astype(BF16)) + _dot(s.astype(BF16), v)
        den = (w_inter * jnp.sum(qh * n0, axis=1, keepdims=True)
               + jnp.sum(s, axis=1, keepdims=True))
        hh = num / jnp.maximum(jnp.abs(den), jnp.exp(-m))
        hn_ref[:, sl] = (hh * _rms_scale(hh, DH_M)) * hg_ref[:, sl]

        m_last = m[ch - 1:ch, :]
        b_last = bc[ch - 1:ch, :]
        w_state = jnp.exp(b_last - bc + igc - m_last)
        decay = jnp.exp(b_last + m0 - m_last)
        kw = kh * w_state
        c_ref[0, hd] = decay * c0 + _dot_tn(kw.astype(BF16), v)
        n_ref[0, hd:hd + 1, :] = decay * n0 + jnp.sum(kw, axis=0, keepdims=True)
        m_ref[0, hd:hd + 1, :] = jnp.broadcast_to(m_last, (1, LANES))

    hm = (hn_ref[...] * og_ref[0].astype(F32)).astype(BF16)
    pm = _dot(hm, wpm_ref[...])
    mm_ref[0] = (sgm_ref[0].astype(F32) * pm).astype(BF16)


def _mlstm(xcm, vm, gates, og, sgm, cbuf, c0, n0, m0b, prm, ch):
    B, L, _ = xcm.shape
    seq = lambda w: pl.BlockSpec((1, ch, w), lambda b, c: (b, c, 0))
    per_b = lambda *s: pl.BlockSpec((1,) + s, lambda b, c: (b,) + (0,) * len(s))
    consts = [prm["b_if"], prm["conv_w"], prm["conv_b"], prm["wq_m"], prm["wk_m"], prm["hnorm_g"],
              prm["w_pm"]]
    return pl.pallas_call(
        functools.partial(_mlstm_kernel, ch=ch),
        grid=(B, L // ch),
        in_specs=[seq(D_MLSTM), seq(D_MLSTM), seq(LANES), seq(D_MLSTM), seq(D_MODEL),
                  per_b(SUBLANES, D_MLSTM), per_b(H_M, DH_M, DH_M), per_b(H_M, DH_M),
                  per_b(H_M, LANES)] + [_const_spec(c.shape) for c in consts],
        out_specs=[seq(D_MODEL), per_b(H_M, DH_M, DH_M), per_b(H_M, DH_M), per_b(H_M, LANES)],
        out_shape=[jax.ShapeDtypeStruct((B, L, D_MODEL), BF16),
                   jax.ShapeDtypeStruct((B, H_M, DH_M, DH_M), F32),
                   jax.ShapeDtypeStruct((B, H_M, DH_M), F32),
                   jax.ShapeDtypeStruct((B, H_M, LANES), F32)],
        scratch_shapes=[pltpu.VMEM((ch + SUBLANES, D_MLSTM), F32),
                        pltpu.VMEM((ch, D_MLSTM), F32)],
        compiler_params=pltpu.CompilerParams(
            dimension_semantics=("arbitrary", "arbitrary"), vmem_limit_bytes=VMEM_LIMIT_BYTES),
        name="mlstm",
    )(xcm, vm, gates, og, sgm, cbuf, c0, n0, m0b, *consts)


def _outproj_kernel(o_ref, sza_ref, sga_ref, mm_ref, x_ref, wpa_ref, wout_ref, y_ref):
    a = (o_ref[...].astype(F32) * sza_ref[...].astype(F32)).astype(BF16)
    pa = _dot(a, wpa_ref[...])
    merged = mm_ref[...].astype(F32) + sga_ref[...].astype(F32) * pa
    y_ref[...] = x_ref[...] + _dot(merged.astype(BF16), wout_ref[...])


def _outproj(o, sza, sga, mm, x2, prm, tm):
    T = x2.shape[0]
    row = pl.BlockSpec((tm, D_MODEL), lambda i: (i, 0))
    return pl.pallas_call(
        _outproj_kernel,
        grid=(T // tm,),
        in_specs=[row] * 5 + [_const_spec(prm["w_pa"].shape), _const_spec(prm["w_out"].shape)],
        out_specs=row,
        out_shape=jax.ShapeDtypeStruct((T, D_MODEL), F32),
        compiler_params=pltpu.CompilerParams(
            dimension_semantics=("arbitrary",), vmem_limit_bytes=VMEM_LIMIT_BYTES),
        name="outproj",
    )(o, sza, sga, mm, x2, prm["w_pa"], prm["w_out"])


def _swap_halves(a):
    half = ROPE // 2
    return jnp.concatenate([a[..., half:], a[..., :half]], axis=-1)


def _prepare_params(norm_g, w_in, b_if, conv_w, conv_b, wq_m, wk_m, hnorm_g, qn_g, w_uq, kvn_g,
                    w_ukv, g_qn, g_qr, g_kn, g_kr, w_pm, w_pa, w_out):
    cols, start = [], 0
    for w in SPLITS:
        cols.append(w_in[:, start:start + w])
        start += w
    xcm, vm, ig, fg, og, zm, cq, ckv, kr, za, gm, ga = cols
    pad = jnp.zeros((D_MODEL, LANES - 2 * H_M), F32)
    w_small = jnp.concatenate([ckv, kr, _swap_halves(kr), ig, fg, pad], axis=1)
    wuq = w_uq.reshape(Q_LORA, H_A, QK_DIM)
    wuq = jnp.concatenate([wuq, _swap_halves(wuq[..., NOPE:])], axis=-1)
    row = lambda a: a.reshape(1, -1).astype(F32)
    pair = lambda g: row(jnp.concatenate([g, _swap_halves(g)]))
    pair2 = lambda g: row(jnp.concatenate([_swap_halves(g), g]))
    return {
        "norm_g": row(norm_g),
        "w_big": jnp.concatenate([xcm, vm, og, zm, za, gm, ga], axis=1).astype(BF16),
        "w_cq": cq.astype(BF16),
        "w_small": w_small.astype(BF16),
        "qn_g": row(qn_g),
        "w_uq": wuq.reshape(Q_LORA, H_A * 2 * LANES).astype(BF16),
        "kvn_g": row(kvn_g),
        "g_kr": pair(g_kr), "g_kr2": pair2(g_kr),
        "g_qn": row(g_qn),
        "g_qr": pair(g_qr), "g_qr2": pair2(g_qr),
        "w_ukv": w_ukv.astype(BF16),
        "g_kn": row(g_kn),
        "q_bound2": (ATT_SCALE * LOG2E) ** 2 * (NOPE * jnp.max(g_qn * g_qn)
                                                + ROPE * jnp.max(g_qr * g_qr)),
        "b_if": row(jnp.concatenate([b_if, jnp.zeros((LANES - 2 * H_M,), F32)])),
        "conv_w": conv_w, "conv_b": row(conv_b),
        "wq_m": wq_m.astype(BF16),
        "wk_m": (wk_m * (DH_M ** -0.5)).astype(BF16),
        "hnorm_g": row(hnorm_g),
        "w_pm": w_pm.astype(BF16), "w_pa": w_pa.astype(BF16), "w_out": w_out.astype(BF16),
    }


def _rope_tables(past, length):
    half = ROPE // 2
    inv = jnp.power(ROPE_THETA, -jnp.arange(half, dtype=F32) / half)
    ang = (past + jnp.arange(length)).astype(F32)[:, None] * inv[None, :]
    cos, sin = jnp.cos(ang), jnp.sin(ang)
    return jnp.tile(cos, (1, 4)), jnp.tile(jnp.concatenate([-sin, sin], axis=1), (1, 2))


def _hybrid_layer(x, past_ckv, past_kr, conv_buf, c0, n0, m0, prm, *, tm, tq, tkv, ch):
    B, L, _ = x.shape
    past = past_ckv.shape[1]
    cos_t, sin_t = _rope_tables(past, L)
    (xcm, vm, og, sza, sgm, sga, gates, ckv_new, kr_new, q, tail) = _inproj(x, prm, cos_t, sin_t, tm)
    seq = lambda a: a.reshape(B, L, a.shape[-1])
    ckv_new, kr_new = seq(ckv_new), seq(kr_new)

    ckv_all = jnp.concatenate([past_ckv, ckv_new], axis=1) if past else ckv_new
    kr_all = jnp.concatenate([past_kr, kr_new], axis=1) if past else kr_new
    T = kr_all.shape[1]
    aug = jnp.concatenate([jnp.ones((B, T, 1), F32), jnp.zeros((B, T, LANES - ROPE - 1), F32)], -1)
    k, v, kmax_tiles = _kvup(ckv_all, jnp.concatenate([kr_all, aug], axis=-1), prm, tkv)
    o = _attention(q, k, v, jnp.max(kmax_tiles[..., 0], axis=1), prm["q_bound2"], tq)

    cbuf = jnp.concatenate(
        [jnp.zeros((B, SUBLANES - (CONV_W - 1), D_MLSTM), F32), conv_buf.astype(F32)], axis=1)
    m0b = jnp.broadcast_to(m0.astype(F32)[:, :, None], (B, H_M, LANES))
    mm, c1, n1, m1 = _mlstm(seq(xcm), seq(vm), seq(gates), seq(og), seq(sgm), cbuf,
                            c0.astype(F32), n0.astype(F32), m0b, prm, ch)

    y = _outproj(o.reshape(B * L, D_ATT), sza, sga, mm.reshape(B * L, D_MODEL),
                 x.reshape(B * L, D_MODEL), prm, tm)
    conv_new = tail[:, SUBLANES - (CONV_W - 1):, :]
    return y.reshape(B, L, D_MODEL), ckv_new, kr_new, conv_new, c1, n1, m1[:, :, 0]


def kernel(x_prompt, x_sample, cache_ckv, cache_kr, state_conv, state_C, state_n, state_m,
           norm_g, w_in, b_if, conv_w, conv_b, wq_m, wk_m, hnorm_g,
           qn_g, w_uq, kvn_g, w_ukv, g_qn, g_qr, g_kn, g_kr, w_pm, w_pa, w_out):
    dt = x_prompt.dtype
    B = x_prompt.shape[0]
    depth = norm_g.shape[0]
    yp, ys = x_prompt, x_sample
    outs_p = [[] for _ in range(6)]
    outs_s = [[] for _ in range(6)]
    for l in range(depth):
        prm = _prepare_params(norm_g[l], w_in[l], b_if[l], conv_w[l], conv_b[l], wq_m[l], wk_m[l],
                              hnorm_g[l], qn_g[l], w_uq[l], kvn_g[l], w_ukv[l], g_qn[l], g_qr[l],
                              g_kn[l], g_kr[l], w_pm[l], w_pa[l], w_out[l])
        yp, *res_p = _hybrid_layer(
            yp, jnp.zeros((B, 0, KV_LORA), dt), jnp.zeros((B, 0, ROPE), dt),
            jnp.zeros((B, CONV_W - 1, D_MLSTM), dt), jnp.zeros((B, H_M, DH_M, DH_M), F32),
            jnp.zeros((B, H_M, DH_M), F32), jnp.zeros((B, H_M), F32), prm,
            tm=256, tq=512, tkv=512, ch=256)
        Ls = x_sample.shape[1]
        ys, *res_s = _hybrid_layer(
            ys, cache_ckv[l], cache_kr[l], state_conv[l], state_C[l], state_n[l], state_m[l], prm,
            tm=Ls, tq=Ls, tkv=cache_ckv.shape[2] + Ls, ch=Ls)
        for lst, a in zip(outs_p, res_p):
            lst.append(a.astype(dt))
        for lst, a in zip(outs_s, res_s):
            lst.append(a.astype(dt))
    return (yp, ys) + tuple(jnp.stack(a) for a in outs_p) + tuple(jnp.stack(a) for a in outs_s)
```

```python
import functools
import math

import jax
import jax.numpy as jnp
from jax import lax
from jax.experimental import pallas as pl
from jax.experimental.pallas import tpu as pltpu

F32 = jnp.float32
BF16 = jnp.bfloat16

D_MODEL = 1024
CHUNK = 64
EPS = 1e-6
NEG = -1e30
D_MLSTM = D_MODEL
H_M = 4
DH_M = D_MLSTM // H_M
CONV_W = 4
H_A = 8
NOPE = 128
ROPE = 64
V_DIM = 128
Q_LORA = 384
KV_LORA = 256
D_ATT = H_A * V_DIM
ROPE_THETA = 10000.0
ATT_SCALE = (NOPE + ROPE) ** -0.5
LOG2E = math.log2(math.e)
QK_DIM = NOPE + ROPE
QK_PAD = 256
V_PAD = 256
SAFE_BOUND = 50.0
SPLITS = (D_MLSTM, D_MLSTM, H_M, H_M, D_MLSTM, D_MLSTM, Q_LORA, KV_LORA, ROPE, D_ATT, D_MODEL, D_MODEL)

LANES = 128
SUBLANES = 8
VMEM_LIMIT_BYTES = 56 * 1024 * 1024


def _const_spec(shape):
    nd = len(shape)
    return pl.BlockSpec(shape, lambda *_: (0,) * nd, pipeline_mode=pl.Buffered(1))


def _dot(a, b):
    return jnp.dot(a, b, preferred_element_type=F32)


def _dot_nt(a, b):
    return lax.dot_general(a, b, (((1,), (1,)), ((), ())), preferred_element_type=F32)


def _dot_tn(a, b):
    return lax.dot_general(a, b, (((0,), (0,)), ((), ())), preferred_element_type=F32)


def _rms_scale(x, width):
    return lax.rsqrt(jnp.sum(x * x, axis=-1, keepdims=True) * (1.0 / width) + EPS)


def _sigmoid(x):
    return 1.0 / (1.0 + jnp.exp(-x))


def _split3(x):
    a = x.astype(BF16)
    r = x - a.astype(F32)
    b = r.astype(BF16)
    c = (r - b.astype(F32)).astype(BF16)
    return a, b, c


def _rope_half_norm(t, gain, gain_rolled, cos_t, sin_t):
    lane = lax.broadcasted_iota(jnp.int32, t.shape, 1)
    sq = jnp.where(lane < ROPE, t * t, 0.0)
    r = lax.rsqrt(jnp.sum(sq, axis=-1, keepdims=True) * (1.0 / ROPE) + EPS)
    t_sw = pltpu.roll(t, ROPE, axis=1)
    return r * (t * gain * cos_t + t_sw * gain_rolled * sin_t)


def _inproj_kernel(x_ref, ng_ref, wbig_ref, wcq_ref, wsm_ref, qng_ref, wuq_ref, kvng_ref,
                   gkr_ref, gkr2_ref, gqn_ref, gqr_ref, gqr2_ref, cos_ref, sin_ref,
                   xcm_ref, vm_ref, og_ref, sza_ref, sgm_ref, sga_ref, gates_ref,
                   ckv_ref, kr_ref, q_ref, tail_ref):
    x = x_ref[...]
    h = ((x * _rms_scale(x, D_MODEL)) * ng_ref[...]).astype(BF16)
    cos_t = cos_ref[...]
    sin_t = sin_ref[...]

    def seg(i):
        return _dot(h, wbig_ref[:, i * D_MODEL:(i + 1) * D_MODEL])

    cq = _dot(h, wcq_ref[...])
    cqn = ((cq * _rms_scale(cq, Q_LORA)) * qng_ref[...]).astype(BF16)
    qscale = ATT_SCALE * LOG2E

    def q_head(hd):
        qh = _dot(cqn, wuq_ref[:, hd * 2 * LANES:(hd + 1) * 2 * LANES])
        qn = qh[:, :NOPE]
        qn = (qn * _rms_scale(qn, NOPE)) * gqn_ref[...]
        q_ref[0, hd, :, :NOPE] = (qn * qscale).astype(BF16)
        qr = _rope_half_norm(qh[:, NOPE:], gqr_ref[...], gqr2_ref[...], cos_t, sin_t)
        lane = lax.broadcasted_iota(jnp.int32, qr.shape, 1)
        q_ref[0, hd, :, NOPE:] = jnp.where(lane < ROPE, qr * qscale, 0.0).astype(BF16)

    t = _dot(h, wsm_ref[...])
    c = t[:, :KV_LORA]
    ckv_ref[...] = (c * _rms_scale(c, KV_LORA)) * kvng_ref[...]
    kr = _rope_half_norm(t[:, KV_LORA:KV_LORA + LANES], gkr_ref[...], gkr2_ref[...], cos_t, sin_t)
    kr_ref[...] = kr[:, :ROPE]
    gates_ref[...] = t[:, KV_LORA + LANES:]

    q_head(0)
    xcm = seg(0)
    xcm_ref[...] = xcm.astype(BF16)
    tail_ref[0] = xcm[xcm.shape[0] - SUBLANES:, :]
    q_head(1)
    vm_ref[...] = seg(1).astype(BF16)
    q_head(2)
    z = seg(3)
    q_head(3)
    og_ref[...] = (_sigmoid(seg(2)) * (z * _sigmoid(z))).astype(BF16)
    q_head(4)
    z = seg(4)
    sza_ref[...] = (z * _sigmoid(z)).astype(BF16)
    q_head(5)
    sgm_ref[...] = _sigmoid(seg(5)).astype(BF16)
    q_head(6)
    q_head(7)
    sga_ref[...] = _sigmoid(seg(6)).astype(BF16)


def _inproj(x, prm, cos_t, sin_t, tm):
    B, L, _ = x.shape
    T = B * L
    tps = L // tm
    x2 = x.reshape(T, D_MODEL)
    row = lambda w: pl.BlockSpec((tm, w), lambda i: (i, 0))
    pos = pl.BlockSpec((tm, LANES), lambda i: (i % tps, 0))
    consts = [prm["norm_g"], prm["w_big"], prm["w_cq"], prm["w_small"], prm["qn_g"], prm["w_uq"],
              prm["kvn_g"], prm["g_kr"], prm["g_kr2"], prm["g_qn"], prm["g_qr"], prm["g_qr2"]]
    wide = jax.ShapeDtypeStruct((T, D_MODEL), BF16)
    out_shape = [wide] * 6 + [
        jax.ShapeDtypeStruct((T, LANES), F32),
        jax.ShapeDtypeStruct((T, KV_LORA), F32),
        jax.ShapeDtypeStruct((T, ROPE), F32),
        jax.ShapeDtypeStruct((B, H_A, L, QK_PAD), BF16),
        jax.ShapeDtypeStruct((B, SUBLANES, D_MLSTM), F32),
    ]
    out_specs = [row(D_MODEL)] * 6 + [
        row(LANES), row(KV_LORA), row(ROPE),
        pl.BlockSpec((1, H_A, tm, QK_PAD), lambda i: (i // tps, 0, i % tps, 0)),
        pl.BlockSpec((1, SUBLANES, D_MLSTM), lambda i: (i // tps, 0, 0)),
    ]
    return pl.pallas_call(
        _inproj_kernel,
        grid=(T // tm,),
        in_specs=[row(D_MODEL)] + [_const_spec(c.shape) for c in consts] + [pos, pos],
        out_specs=out_specs,
        out_shape=out_shape,
        compiler_params=pltpu.CompilerParams(
            dimension_semantics=("arbitrary",), vmem_limit_bytes=VMEM_LIMIT_BYTES),
        name="inproj",
    )(x2, *consts, cos_t, sin_t)


def _kvup_kernel(ckv_ref, kr_ref, w_ref, gkn_ref, k_ref, v_ref, kmax_ref):
    c = ckv_ref[0].astype(BF16)
    kr = kr_ref[0]
    kr2 = jnp.sum(kr * kr, axis=1, keepdims=True)
    lane = lax.broadcasted_iota(jnp.int32, (c.shape[0], 2 * LANES), 1)
    for hd in range(H_A):
        kv = _dot(c, w_ref[:, hd * 2 * LANES:(hd + 1) * 2 * LANES])
        kn = kv[:, :NOPE]
        kn = (kn * _rms_scale(kn, NOPE)) * gkn_ref[...]
        k_ref[0, hd, :, :NOPE] = kn.astype(BF16)
        k_ref[0, hd, :, NOPE:] = kr.astype(BF16)
        v_aug = jnp.where(lane < V_DIM, pltpu.roll(kv, V_DIM, axis=1),
                          (lane == V_DIM).astype(F32))
        v_ref[0, hd] = v_aug.astype(BF16)
    kmax_ref[0, 0] = jnp.broadcast_to(jnp.max(kr2, axis=0, keepdims=True), (SUBLANES, LANES))


def _kvup(ckv, kr, prm, tm):
    B, T, _ = ckv.shape
    return pl.pallas_call(
        _kvup_kernel,
        grid=(B, T // tm),
        in_specs=[pl.BlockSpec((1, tm, KV_LORA), lambda b, i: (b, i, 0)),
                  pl.BlockSpec((1, tm, LANES), lambda b, i: (b, i, 0)),
                  _const_spec(prm["w_ukv"].shape), _const_spec(prm["g_kn"].shape)],
        out_specs=[pl.BlockSpec((1, H_A, tm, QK_PAD), lambda b, i: (b, 0, i, 0)),
                   pl.BlockSpec((1, H_A, tm, V_PAD), lambda b, i: (b, 0, i, 0)),
                   pl.BlockSpec((1, 1, SUBLANES, LANES), lambda b, i: (b, i, 0, 0))],
        out_shape=[jax.ShapeDtypeStruct((B, H_A, T, QK_PAD), BF16),
                   jax.ShapeDtypeStruct((B, H_A, T, V_PAD), BF16),
                   jax.ShapeDtypeStruct((B, T // tm, SUBLANES, LANES), F32)],
        compiler_params=pltpu.CompilerParams(
            dimension_semantics=("arbitrary", "arbitrary"), vmem_limit_bytes=VMEM_LIMIT_BYTES),
        name="kvup",
    )(ckv, kr, prm["w_ukv"], prm["g_kn"])


HEADS_PER_STEP = 2


def _attn_kernel(safe_ref, q_ref, k_ref, v_ref, negb_ref, o_ref, acc_ref, qa_ref, m_ref,
                 *, tq, wide, past):
    b, hp, qi = pl.program_id(0), pl.program_id(1), pl.program_id(2)
    diag0 = past + qi * tq
    heads = range(HEADS_PER_STEP)

    def chunk_mask(width):
        rows = lax.broadcasted_iota(jnp.int32, (tq, width), 0) // CHUNK
        cols = lax.broadcasted_iota(jnp.int32, (tq, width), 1) // CHUNK
        return cols <= rows + (width - tq) // CHUNK

    acc_ref[...] = jnp.zeros_like(acc_ref)

    @pl.when(safe_ref[b, hp] == 1)
    def _bounded():
        lane = lax.broadcasted_iota(jnp.int32, (tq, QK_PAD), 1)
        for h in heads:
            qa_ref[h] = jnp.where(lane == QK_DIM, negb_ref[0, h], q_ref[0, h].astype(F32)
                                  ).astype(BF16)

        def step(ks, size, msk):
            for h in heads:
                s = _dot_nt(qa_ref[h], k_ref[0, h, pl.ds(ks, size), :])
                if msk is not None:
                    s = jnp.where(msk, s, NEG)
                p = jnp.exp2(s).astype(BF16)
                acc_ref[h] += _dot(p, v_ref[0, h, pl.ds(ks, size), :])

        @pl.loop(0, diag0 // wide)
        def _(j):
            step(pl.multiple_of(j * wide, wide), wide, None)

        @pl.when(diag0 % wide == 0)
        def _():
            step(pl.multiple_of(diag0, tq), tq, chunk_mask(tq))

        @pl.when(diag0 % wide != 0)
        def _():
            step(pl.multiple_of(diag0 - tq, tq), 2 * tq, chunk_mask(2 * tq))

    @pl.when(safe_ref[b, hp] == 0)
    def _online():
        m_ref[...] = jnp.full_like(m_ref, NEG)

        def step(ks, msk):
            for h in heads:
                s = _dot_nt(q_ref[0, h], k_ref[0, h, pl.ds(ks, tq), :])
                if msk is not None:
                    s = jnp.where(msk, s, NEG)
                m_old = m_ref[h]
                m_new = jnp.maximum(m_old, jnp.max(s, axis=1, keepdims=True))
                p = jnp.exp2(s - m_new).astype(BF16)
                acc_ref[h] = (jnp.exp2(m_old - m_new) * acc_ref[h]
                              + _dot(p, v_ref[0, h, pl.ds(ks, tq), :]))
                m_ref[h] = m_new

        @pl.loop(0, diag0 // tq)
        def _(j):
            step(pl.multiple_of(j * tq, tq), None)

        step(pl.multiple_of(diag0, tq), chunk_mask(tq))

    for h in heads:
        acc = acc_ref[h]
        o_ref[0, :, h * V_DIM:(h + 1) * V_DIM] = (
            acc[:, :V_DIM] / acc[:, V_DIM:V_DIM + 1]).astype(BF16)


def _attention(q, k, v, kmax2, q_bound2, tq, wide):
    B, H, L, _ = q.shape
    T = k.shape[2]
    past = T - L
    hp = HEADS_PER_STEP
    assert past % wide == 0 and (wide == 2 * tq or L == tq) and L % tq == 0 and H % hp == 0
    bound = jnp.sqrt(kmax2 * q_bound2) * 1.01
    safe = jnp.all(bound.reshape(B, H // hp, hp) <= SAFE_BOUND, axis=-1).astype(jnp.int32)
    negb = jnp.broadcast_to(-bound[:, :, None, None], (B, H, 1, QK_PAD))
    grid_spec = pltpu.PrefetchScalarGridSpec(
        num_scalar_prefetch=1,
        grid=(B, H // hp, L // tq),
        in_specs=[pl.BlockSpec((1, hp, tq, QK_PAD), lambda b, h, i, s: (b, h, i, 0)),
                  pl.BlockSpec((1, hp, T, QK_PAD), lambda b, h, i, s: (b, h, 0, 0)),
                  pl.BlockSpec((1, hp, T, V_PAD), lambda b, h, i, s: (b, h, 0, 0)),
                  pl.BlockSpec((1, hp, 1, QK_PAD), lambda b, h, i, s: (b, h, 0, 0))],
        out_specs=pl.BlockSpec((1, tq, hp * V_DIM), lambda b, h, i, s: (b, i, h)),
        scratch_shapes=[pltpu.VMEM((hp, tq, V_PAD), F32), pltpu.VMEM((hp, tq, QK_PAD), BF16),
                        pltpu.VMEM((hp, tq, 1), F32)],
    )
    return pl.pallas_call(
        functools.partial(_attn_kernel, tq=tq, wide=wide, past=past),
        grid_spec=grid_spec,
        out_shape=jax.ShapeDtypeStruct((B, L, D_ATT), BF16),
        compiler_params=pltpu.CompilerParams(
            dimension_semantics=("arbitrary", "arbitrary", "arbitrary"),
            vmem_limit_bytes=VMEM_LIMIT_BYTES),
        name="attention",
    )(safe, q, k, v, negb)


def _mlstm_kernel(xcm_ref, vm_ref, gates_ref, og_ref, sgm_ref, o_ref, sza_ref, sga_ref, x_ref,
                  cbuf_ref, c0_ref, n0_ref, m0_ref,
                  bif_ref, cw_ref, cb_ref, wq_ref, wk_ref, hg_ref, wpm_ref, wpa_ref, wout_ref,
                  y_ref, c_ref, n_ref, m_ref, xpad_ref, hn_ref, *, ch):
    ci = pl.program_id(1)
    pa = _dot((o_ref[0].astype(F32) * sza_ref[0].astype(F32)).astype(BF16), wpa_ref[...])

    @pl.when(ci == 0)
    def _():
        c_ref[...] = c0_ref[...]
        n_ref[...] = n0_ref[...]
        m_ref[...] = m0_ref[...]
        xpad_ref[0:SUBLANES, :] = cbuf_ref[0]

    xpad_ref[SUBLANES:, :] = xcm_ref[0].astype(F32)
    u = cb_ref[...] + xpad_ref[SUBLANES:, :] * cw_ref[CONV_W - 1:CONV_W, :]
    for j in range(CONV_W - 1):
        off = SUBLANES - (CONV_W - 1) + j
        u = u + xpad_ref[off:off + ch, :] * cw_ref[j:j + 1, :]
    xpad_ref[0:SUBLANES, :] = xpad_ref[ch:ch + SUBLANES, :]
    su = (u * _sigmoid(u)).astype(BF16)

    gpre = gates_ref[0] + bif_ref[...]
    lane = lax.broadcasted_iota(jnp.int32, (ch, LANES), 1)
    lf = jnp.minimum(gpre, 0.0) - jnp.log1p(jnp.exp(-jnp.abs(gpre)))
    lf = jnp.where((lane >= H_M) & (lane < 2 * H_M), lf, 0.0)
    tri = (lax.broadcasted_iota(jnp.int32, (ch, ch), 1)
           <= lax.broadcasted_iota(jnp.int32, (ch, ch), 0))
    tri_b = tri.astype(BF16)
    bcum = sum(_dot(tri_b, part) for part in _split3(lf))
    xparts = _split3(jnp.where(lane < H_M, gpre, -bcum))

    for hd in range(H_M):
        sl = slice(hd * DH_M, (hd + 1) * DH_M)
        m0 = m_ref[0, hd:hd + 1, 0:1]
        bc = bcum[:, H_M + hd:H_M + hd + 1]
        igc = gpre[:, hd:hd + 1]
        pick = ((lane == hd) | (lane == H_M + hd)).astype(BF16)
        drow = sum(_dot_nt(pick, part) for part in xparts)
        dlog = jnp.where(tri, bc + drow, NEG)
        g = bc + m0
        m = jnp.maximum(g, jnp.max(dlog, axis=1, keepdims=True))
        w_intra = jnp.exp(dlog - m)
        w_inter = jnp.exp(g - m)

        su_h = su[:, sl]
        qh = _dot(su_h, wq_ref[hd])
        kh = _dot(su_h, wk_ref[hd])
        qb = qh.astype(BF16)
        v = vm_ref[0, :, sl]
        c0 = c_ref[0, hd]
        n0 = n_ref[0, hd:hd + 1, :]
        s = _dot_nt(qb, kh.astype(BF16)) * w_intra
        num = w_inter * _dot(qb, c0.astype(BF16)) + _dot(s.astype(BF16), v)
        den = (w_inter * jnp.sum(qh * n0, axis=1, keepdims=True)
               + jnp.sum(s, axis=1, keepdims=True))
        hh = num / jnp.maximum(jnp.abs(den), jnp.exp(-m))
        hn_ref[:, sl] = (hh * _rms_scale(hh, DH_M)) * hg_ref[:, sl]

        m_last = m[ch - 1:ch, :]
        b_last = bc[ch - 1:ch, :]
        w_state = jnp.exp(b_last - bc + igc - m_last)
        decay = jnp.exp(b_last + m0 - m_last)
        kw = kh * w_state
        c_ref[0, hd] = decay * c0 + _dot_tn(kw.astype(BF16), v)
        n_ref[0, hd:hd + 1, :] = decay * n0 + jnp.sum(kw, axis=0, keepdims=True)
        m_ref[0, hd:hd + 1, :] = jnp.broadcast_to(m_last, (1, LANES))

    hm = (hn_ref[...] * og_ref[0].astype(F32)).astype(BF16)
    pm = _dot(hm, wpm_ref[...])
    merged = sgm_ref[0].astype(F32) * pm + sga_ref[0].astype(F32) * pa
    y_ref[0] = x_ref[0] + _dot(merged.astype(BF16), wout_ref[...])


def _mlstm(xcm, vm, gates, og, sgm, o, sza, sga, x, cbuf, c0, n0, m0b, prm, ch):
    B, L, _ = xcm.shape
    seq = lambda w: pl.BlockSpec((1, ch, w), lambda b, c: (b, c, 0))
    per_b = lambda *s: pl.BlockSpec((1,) + s, lambda b, c: (b,) + (0,) * len(s))
    consts = [prm["b_if"], prm["conv_w"], prm["conv_b"], prm["wq_m"], prm["wk_m"], prm["hnorm_g"],
              prm["w_pm"], prm["w_pa"], prm["w_out"]]
    return pl.pallas_call(
        functools.partial(_mlstm_kernel, ch=ch),
        grid=(B, L // ch),
        in_specs=[seq(D_MLSTM), seq(D_MLSTM), seq(LANES), seq(D_MLSTM), seq(D_MODEL),
                  seq(D_ATT), seq(D_ATT), seq(D_MODEL), seq(D_MODEL),
                  per_b(SUBLANES, D_MLSTM), per_b(H_M, DH_M, DH_M), per_b(H_M, DH_M),
                  per_b(H_M, LANES)] + [_const_spec(c.shape) for c in consts],
        out_specs=[seq(D_MODEL), per_b(H_M, DH_M, DH_M), per_b(H_M, DH_M), per_b(H_M, LANES)],
        out_shape=[jax.ShapeDtypeStruct((B, L, D_MODEL), F32),
                   jax.ShapeDtypeStruct((B, H_M, DH_M, DH_M), F32),
                   jax.ShapeDtypeStruct((B, H_M, DH_M), F32),
                   jax.ShapeDtypeStruct((B, H_M, LANES), F32)],
        scratch_shapes=[pltpu.VMEM((ch + SUBLANES, D_MLSTM), F32),
                        pltpu.VMEM((ch, D_MLSTM), F32)],
        compiler_params=pltpu.CompilerParams(
            dimension_semantics=("arbitrary", "arbitrary"), vmem_limit_bytes=VMEM_LIMIT_BYTES),
        name="mlstm",
    )(xcm, vm, gates, og, sgm, o, sza, sga, x, cbuf, c0, n0, m0b, *consts)


def _swap_halves(a):
    half = ROPE // 2
    return jnp.concatenate([a[..., half:], a[..., :half]], axis=-1)


def _prepare_params(norm_g, w_in, b_if, conv_w, conv_b, wq_m, wk_m, hnorm_g, qn_g, w_uq, kvn_g,
                    w_ukv, g_qn, g_qr, g_kn, g_kr, w_pm, w_pa, w_out):
    cols, start = [], 0
    for w in SPLITS:
        cols.append(w_in[:, start:start + w])
        start += w
    xcm, vm, ig, fg, og, zm, cq, ckv, kr, za, gm, ga = cols
    pad = jnp.zeros((D_MODEL, LANES - 2 * H_M), F32)
    w_small = jnp.concatenate([ckv, kr, _swap_halves(kr), ig, fg, pad], axis=1)
    wuq = w_uq.reshape(Q_LORA, H_A, QK_DIM)
    wuq = jnp.concatenate([wuq, _swap_halves(wuq[..., NOPE:])], axis=-1)
    row = lambda a: a.reshape(1, -1).astype(F32)
    pair = lambda g: row(jnp.concatenate([g, _swap_halves(g)]))
    pair2 = lambda g: row(jnp.concatenate([_swap_halves(g), g]))
    return {
        "norm_g": row(norm_g),
        "w_big": jnp.concatenate([xcm, vm, og, zm, za, gm, ga], axis=1).astype(BF16),
        "w_cq": cq.astype(BF16),
        "w_small": w_small.astype(BF16),
        "qn_g": row(qn_g),
        "w_uq": wuq.reshape(Q_LORA, H_A * 2 * LANES).astype(BF16),
        "kvn_g": row(kvn_g),
        "g_kr": pair(g_kr), "g_kr2": pair2(g_kr),
        "g_qn": row(g_qn),
        "g_qr": pair(g_qr), "g_qr2": pair2(g_qr),
        "w_ukv": w_ukv.astype(BF16),
        "g_kn": row(g_kn),
        "q_bound2": (ATT_SCALE * LOG2E) ** 2 * (NOPE * jnp.max(g_qn * g_qn)
                                                + ROPE * jnp.max(g_qr * g_qr)),
        "kn_bound2": NOPE * jnp.max(g_kn * g_kn),
        "b_if": row(jnp.concatenate([b_if, jnp.zeros((LANES - 2 * H_M,), F32)])),
        "conv_w": conv_w, "conv_b": row(conv_b),
        "wq_m": wq_m.astype(BF16),
        "wk_m": (wk_m * (DH_M ** -0.5)).astype(BF16),
        "hnorm_g": row(hnorm_g),
        "w_pm": w_pm.astype(BF16), "w_pa": w_pa.astype(BF16), "w_out": w_out.astype(BF16),
    }


def _rope_tables(past, length):
    half = ROPE // 2
    inv = jnp.power(ROPE_THETA, -jnp.arange(half, dtype=F32) / half)
    ang = (past + jnp.arange(length)).astype(F32)[:, None] * inv[None, :]
    cos, sin = jnp.cos(ang), jnp.sin(ang)
    return jnp.tile(cos, (1, 4)), jnp.tile(jnp.concatenate([-sin, sin], axis=1), (1, 2))


def _hybrid_layer(x, past_ckv, past_kr, conv_buf, c0, n0, m0, prm, *, tm, tq, wide, tkv, ch):
    B, L, _ = x.shape
    past = past_ckv.shape[1]
    cos_t, sin_t = _rope_tables(past, L)
    (xcm, vm, og, sza, sgm, sga, gates, ckv_new, kr_new, q, tail) = _inproj(x, prm, cos_t, sin_t, tm)
    seq = lambda a: a.reshape(B, L, a.shape[-1])
    ckv_new, kr_new = seq(ckv_new), seq(kr_new)

    ckv_all = jnp.concatenate([past_ckv, ckv_new], axis=1) if past else ckv_new
    kr_all = jnp.concatenate([past_kr, kr_new], axis=1) if past else kr_new
    T = kr_all.shape[1]
    aug = jnp.concatenate([jnp.ones((B, T, 1), F32), jnp.zeros((B, T, LANES - ROPE - 1), F32)], -1)
    k, v, kmax_tiles = _kvup(ckv_all, jnp.concatenate([kr_all, aug], axis=-1), prm, tkv)
    kmax2 = jnp.max(kmax_tiles[:, :, 0, 0], axis=1) + prm["kn_bound2"]
    o = _attention(q, k, v, jnp.broadcast_to(kmax2[:, None], (B, H_A)), prm["q_bound2"], tq,
                   wide)

    cbuf = jnp.concatenate(
        [jnp.zeros((B, SUBLANES - (CONV_W - 1), D_MLSTM), F32), conv_buf.astype(F32)], axis=1)
    m0b = jnp.broadcast_to(m0.astype(F32)[:, :, None], (B, H_M, LANES))
    y, c1, n1, m1 = _mlstm(seq(xcm), seq(vm), seq(gates), seq(og), seq(sgm), o, seq(sza), seq(sga),
                           x, cbuf, c0.astype(F32), n0.astype(F32), m0b, prm, ch)
    conv_new = tail[:, SUBLANES - (CONV_W - 1):, :]
    return y, ckv_new, kr_new, conv_new, c1, n1, m1[:, :, 0]


def kernel(x_prompt, x_sample, cache_ckv, cache_kr, state_conv, state_C, state_n, state_m,
           norm_g, w_in, b_if, conv_w, conv_b, wq_m, wk_m, hnorm_g,
           qn_g, w_uq, kvn_g, w_ukv, g_qn, g_qr, g_kn, g_kr, w_pm, w_pa, w_out):
    dt = x_prompt.dtype
    B = x_prompt.shape[0]
    depth = norm_g.shape[0]
    yp, ys = x_prompt, x_sample
    outs_p = [[] for _ in range(6)]
    outs_s = [[] for _ in range(6)]
    for l in range(depth):
        prm = _prepare_params(norm_g[l], w_in[l], b_if[l], conv_w[l], conv_b[l], wq_m[l], wk_m[l],
                              hnorm_g[l], qn_g[l], w_uq[l], kvn_g[l], w_ukv[l], g_qn[l], g_qr[l],
                              g_kn[l], g_kr[l], w_pm[l], w_pa[l], w_out[l])
        yp, *res_p = _hybrid_layer(
            yp, jnp.zeros((B, 0, KV_LORA), dt), jnp.zeros((B, 0, ROPE), dt),
            jnp.zeros((B, CONV_W - 1, D_MLSTM), dt), jnp.zeros((B, H_M, DH_M, DH_M), F32),
            jnp.zeros((B, H_M, DH_M), F32), jnp.zeros((B, H_M), F32), prm,
            tm=512, tq=512, wide=1024, tkv=512, ch=256)
        Ls = x_sample.shape[1]
        ys, *res_s = _hybrid_layer(
            ys, cache_ckv[l], cache_kr[l], state_conv[l], state_C[l], state_n[l], state_m[l], prm,
            tm=Ls, tq=Ls, wide=cache_ckv.shape[2], tkv=cache_ckv.shape[2] + Ls, ch=Ls)
        for lst, a in zip(outs_p, res_p):
            lst.append(a.astype(dt))
        for lst, a in zip(outs_s, res_s):
            lst.append(a.astype(dt))
    return (yp, ys) + tuple(jnp.stack(a) for a in outs_p) + tuple(jnp.stack(a) for a in outs_s)
```

```python
import functools
import math

import jax
import jax.numpy as jnp
from jax import lax
from jax.experimental import pallas as pl
from jax.experimental.pallas import tpu as pltpu

F32 = jnp.float32
BF16 = jnp.bfloat16

D_MODEL = 1024
CHUNK = 64
EPS = 1e-6
NEG = -1e30
D_MLSTM = D_MODEL
H_M = 4
DH_M = D_MLSTM // H_M
CONV_W = 4
H_A = 8
NOPE = 128
ROPE = 64
V_DIM = 128
Q_LORA = 384
KV_LORA = 256
D_ATT = H_A * V_DIM
ROPE_THETA = 10000.0
ATT_SCALE = (NOPE + ROPE) ** -0.5
LOG2E = math.log2(math.e)
QK_DIM = NOPE + ROPE
QK_PAD = 256
V_PAD = 256
SAFE_BOUND = 50.0
SPLITS = (D_MLSTM, D_MLSTM, H_M, H_M, D_MLSTM, D_MLSTM, Q_LORA, KV_LORA, ROPE, D_ATT, D_MODEL, D_MODEL)

LANES = 128
SUBLANES = 8
VMEM_LIMIT_BYTES = 56 * 1024 * 1024


def _const_spec(shape):
    nd = len(shape)
    return pl.BlockSpec(shape, lambda *_: (0,) * nd, pipeline_mode=pl.Buffered(1))


def _dot(a, b):
    return jnp.dot(a, b, preferred_element_type=F32)


def _dot_nt(a, b):
    return lax.dot_general(a, b, (((1,), (1,)), ((), ())), preferred_element_type=F32)


def _dot_tn(a, b):
    return lax.dot_general(a, b, (((0,), (0,)), ((), ())), preferred_element_type=F32)


def _rms_scale(x, width):
    return lax.rsqrt(jnp.sum(x * x, axis=-1, keepdims=True) * (1.0 / width) + EPS)


def _sigmoid(x):
    return 1.0 / (1.0 + jnp.exp(-x))


def _split3(x):
    a = x.astype(BF16)
    r = x - a.astype(F32)
    b = r.astype(BF16)
    c = (r - b.astype(F32)).astype(BF16)
    return a, b, c


def _rope_half_norm(t, gain, gain_rolled, cos_t, sin_t):
    lane = lax.broadcasted_iota(jnp.int32, t.shape, 1)
    sq = jnp.where(lane < ROPE, t * t, 0.0)
    r = lax.rsqrt(jnp.sum(sq, axis=-1, keepdims=True) * (1.0 / ROPE) + EPS)
    t_sw = pltpu.roll(t, ROPE, axis=1)
    return r * (t * gain * cos_t + t_sw * gain_rolled * sin_t)


def _inproj_kernel(x_ref, ng_ref, wbig_ref, wcq_ref, wsm_ref, qng_ref, wuq_ref, kvng_ref,
                   gkr_ref, gkr2_ref, gqn_ref, gqr_ref, gqr2_ref, cos_ref, sin_ref,
                   xcm_ref, vm_ref, og_ref, sza_ref, sgm_ref, sga_ref, gates_ref,
                   ckv_ref, kr_ref, q_ref, tail_ref):
    x = x_ref[...]
    h = ((x * _rms_scale(x, D_MODEL)) * ng_ref[...]).astype(BF16)
    cos_t = cos_ref[...]
    sin_t = sin_ref[...]

    def seg(i):
        return _dot(h, wbig_ref[:, i * D_MODEL:(i + 1) * D_MODEL])

    cq = _dot(h, wcq_ref[...])
    cqn = ((cq * _rms_scale(cq, Q_LORA)) * qng_ref[...]).astype(BF16)
    qscale = ATT_SCALE * LOG2E

    def q_head(hd):
        qh = _dot(cqn, wuq_ref[:, hd * 2 * LANES:(hd + 1) * 2 * LANES])
        qn = qh[:, :NOPE]
        qn = (qn * _rms_scale(qn, NOPE)) * gqn_ref[...]
        q_ref[0, hd, :, :NOPE] = (qn * qscale).astype(BF16)
        qr = _rope_half_norm(qh[:, NOPE:], gqr_ref[...], gqr2_ref[...], cos_t, sin_t)
        lane = lax.broadcasted_iota(jnp.int32, qr.shape, 1)
        q_ref[0, hd, :, NOPE:] = jnp.where(lane < ROPE, qr * qscale, 0.0).astype(BF16)

    t = _dot(h, wsm_ref[...])
    c = t[:, :KV_LORA]
    ckv_ref[...] = (c * _rms_scale(c, KV_LORA)) * kvng_ref[...]
    kr = _rope_half_norm(t[:, KV_LORA:KV_LORA + LANES], gkr_ref[...], gkr2_ref[...], cos_t, sin_t)
    kr_ref[...] = kr[:, :ROPE]
    gates_ref[...] = t[:, KV_LORA + LANES:]

    q_head(0)
    xcm = seg(0)
    xcm_ref[...] = xcm.astype(BF16)
    tail_ref[0] = xcm[xcm.shape[0] - SUBLANES:, :]
    q_head(1)
    vm_ref[...] = seg(1).astype(BF16)
    q_head(2)
    z = seg(3)
    q_head(3)
    og_ref[...] = (_sigmoid(seg(2)) * (z * _sigmoid(z))).astype(BF16)
    q_head(4)
    z = seg(4)
    sza_ref[...] = (z * _sigmoid(z)).astype(BF16)
    q_head(5)
    sgm_ref[...] = _sigmoid(seg(5)).astype(BF16)
    q_head(6)
    q_head(7)
    sga_ref[...] = _sigmoid(seg(6)).astype(BF16)


def _inproj(x, prm, cos_t, sin_t, tm):
    B, L, _ = x.shape
    T = B * L
    tps = L // tm
    x2 = x.reshape(T, D_MODEL)
    row = lambda w: pl.BlockSpec((tm, w), lambda i: (i, 0))
    pos = pl.BlockSpec((tm, LANES), lambda i: (i % tps, 0))
    consts = [prm["norm_g"], prm["w_big"], prm["w_cq"], prm["w_small"], prm["qn_g"], prm["w_uq"],
              prm["kvn_g"], prm["g_kr"], prm["g_kr2"], prm["g_qn"], prm["g_qr"], prm["g_qr2"]]
    wide = jax.ShapeDtypeStruct((T, D_MODEL), BF16)
    out_shape = [wide] * 6 + [
        jax.ShapeDtypeStruct((T, LANES), F32),
        jax.ShapeDtypeStruct((T, KV_LORA), F32),
        jax.ShapeDtypeStruct((T, ROPE), F32),
        jax.ShapeDtypeStruct((B, H_A, L, QK_PAD), BF16),
        jax.ShapeDtypeStruct((B, SUBLANES, D_MLSTM), F32),
    ]
    out_specs = [row(D_MODEL)] * 6 + [
        row(LANES), row(KV_LORA), row(ROPE),
        pl.BlockSpec((1, H_A, tm, QK_PAD), lambda i: (i // tps, 0, i % tps, 0)),
        pl.BlockSpec((1, SUBLANES, D_MLSTM), lambda i: (i // tps, 0, 0)),
    ]
    return pl.pallas_call(
        _inproj_kernel,
        grid=(T // tm,),
        in_specs=[row(D_MODEL)] + [_const_spec(c.shape) for c in consts] + [pos, pos],
        out_specs=out_specs,
        out_shape=out_shape,
        compiler_params=pltpu.CompilerParams(
            dimension_semantics=("arbitrary",), vmem_limit_bytes=VMEM_LIMIT_BYTES),
        name="inproj",
    )(x2, *consts, cos_t, sin_t)


def _kvup_kernel(ckv_ref, kr_ref, w_ref, gkn_ref, k_ref, v_ref, kmax_ref):
    c = ckv_ref[0].astype(BF16)
    kr = kr_ref[0]
    kr2 = jnp.sum(kr * kr, axis=1, keepdims=True)
    lane = lax.broadcasted_iota(jnp.int32, (c.shape[0], 2 * LANES), 1)
    for hd in range(H_A):
        kv = _dot(c, w_ref[:, hd * 2 * LANES:(hd + 1) * 2 * LANES])
        kn = kv[:, :NOPE]
        kn = (kn * _rms_scale(kn, NOPE)) * gkn_ref[...]
        k_ref[0, hd, :, :NOPE] = kn.astype(BF16)
        k_ref[0, hd, :, NOPE:] = kr.astype(BF16)
        v_aug = jnp.where(lane < V_DIM, pltpu.roll(kv, V_DIM, axis=1),
                          (lane == V_DIM).astype(F32))
        v_ref[0, hd] = v_aug.astype(BF16)
    kmax_ref[0, 0] = jnp.broadcast_to(jnp.max(kr2, axis=0, keepdims=True), (SUBLANES, LANES))


def _kvup(ckv, kr, prm, tm):
    B, T, _ = ckv.shape
    return pl.pallas_call(
        _kvup_kernel,
        grid=(B, T // tm),
        in_specs=[pl.BlockSpec((1, tm, KV_LORA), lambda b, i: (b, i, 0)),
                  pl.BlockSpec((1, tm, LANES), lambda b, i: (b, i, 0)),
                  _const_spec(prm["w_ukv"].shape), _const_spec(prm["g_kn"].shape)],
        out_specs=[pl.BlockSpec((1, H_A, tm, QK_PAD), lambda b, i: (b, 0, i, 0)),
                   pl.BlockSpec((1, H_A, tm, V_PAD), lambda b, i: (b, 0, i, 0)),
                   pl.BlockSpec((1, 1, SUBLANES, LANES), lambda b, i: (b, i, 0, 0))],
        out_shape=[jax.ShapeDtypeStruct((B, H_A, T, QK_PAD), BF16),
                   jax.ShapeDtypeStruct((B, H_A, T, V_PAD), BF16),
                   jax.ShapeDtypeStruct((B, T // tm, SUBLANES, LANES), F32)],
        compiler_params=pltpu.CompilerParams(
            dimension_semantics=("arbitrary", "arbitrary"), vmem_limit_bytes=VMEM_LIMIT_BYTES),
        name="kvup",
    )(ckv, kr, prm["w_ukv"], prm["g_kn"])


HEADS_PER_STEP = 2


def _attn_kernel(safe_ref, q_ref, k_ref, v_ref, negb_ref, o_ref, acc_ref, qa_ref, m_ref,
                 *, tq, wide, past):
    b, hp, qi = pl.program_id(0), pl.program_id(1), pl.program_id(2)
    diag0 = past + qi * tq
    heads = range(HEADS_PER_STEP)

    def chunk_mask(width):
        rows = lax.broadcasted_iota(jnp.int32, (tq, width), 0) // CHUNK
        cols = lax.broadcasted_iota(jnp.int32, (tq, width), 1) // CHUNK
        return cols <= rows + (width - tq) // CHUNK

    acc_ref[...] = jnp.zeros_like(acc_ref)

    @pl.when(safe_ref[b, hp] == 1)
    def _bounded():
        lane = lax.broadcasted_iota(jnp.int32, (tq, QK_PAD), 1)
        for h in heads:
            qa_ref[h] = jnp.where(lane == QK_DIM, negb_ref[0, h], q_ref[0, h].astype(F32)
                                  ).astype(BF16)

        def step(ks, size, msk):
            scores = [_dot_nt(qa_ref[h], k_ref[0, h, pl.ds(ks, size), :]) for h in heads]
            for h in heads:
                s = scores[h] if msk is None else jnp.where(msk, scores[h], NEG)
                p = jnp.exp2(s).astype(BF16)
                acc_ref[h] += _dot(p, v_ref[0, h, pl.ds(ks, size), :])

        @pl.loop(0, diag0 // wide)
        def _(j):
            step(pl.multiple_of(j * wide, wide), wide, None)

        @pl.when(diag0 % wide == 0)
        def _():
            step(pl.multiple_of(diag0, tq), tq, chunk_mask(tq))

        @pl.when(diag0 % wide != 0)
        def _():
            step(pl.multiple_of(diag0 - tq, tq), 2 * tq, chunk_mask(2 * tq))

    @pl.when(safe_ref[b, hp] == 0)
    def _online():
        m_ref[...] = jnp.full_like(m_ref, NEG)

        def step(ks, msk):
            for h in heads:
                s = _dot_nt(q_ref[0, h], k_ref[0, h, pl.ds(ks, tq), :])
                if msk is not None:
                    s = jnp.where(msk, s, NEG)
                m_old = m_ref[h]
                m_new = jnp.maximum(m_old, jnp.max(s, axis=1, keepdims=True))
                p = jnp.exp2(s - m_new).astype(BF16)
                acc_ref[h] = (jnp.exp2(m_old - m_new) * acc_ref[h]
                              + _dot(p, v_ref[0, h, pl.ds(ks, tq), :]))
                m_ref[h] = m_new

        @pl.loop(0, diag0 // tq)
        def _(j):
            step(pl.multiple_of(j * tq, tq), None)

        step(pl.multiple_of(diag0, tq), chunk_mask(tq))

    for h in heads:
        acc = acc_ref[h]
        o_ref[0, :, h * V_DIM:(h + 1) * V_DIM] = (
            acc[:, :V_DIM] / acc[:, V_DIM:V_DIM + 1]).astype(BF16)


def _attention(q, k, v, kmax2, q_bound2, tq, wide):
    B, H, L, _ = q.shape
    T = k.shape[2]
    past = T - L
    hp = HEADS_PER_STEP
    assert past % wide == 0 and (wide == 2 * tq or L == tq) and L % tq == 0 and H % hp == 0
    bound = jnp.sqrt(kmax2 * q_bound2) * 1.01
    safe = jnp.all(bound.reshape(B, H // hp, hp) <= SAFE_BOUND, axis=-1).astype(jnp.int32)
    negb = jnp.broadcast_to(-bound[:, :, None, None], (B, H, 1, QK_PAD))
    grid_spec = pltpu.PrefetchScalarGridSpec(
        num_scalar_prefetch=1,
        grid=(B, H // hp, L // tq),
        in_specs=[pl.BlockSpec((1, hp, tq, QK_PAD), lambda b, h, i, s: (b, h, i, 0)),
                  pl.BlockSpec((1, hp, T, QK_PAD), lambda b, h, i, s: (b, h, 0, 0)),
                  pl.BlockSpec((1, hp, T, V_PAD), lambda b, h, i, s: (b, h, 0, 0)),
                  pl.BlockSpec((1, hp, 1, QK_PAD), lambda b, h, i, s: (b, h, 0, 0))],
        out_specs=pl.BlockSpec((1, tq, hp * V_DIM), lambda b, h, i, s: (b, i, h)),
        scratch_shapes=[pltpu.VMEM((hp, tq, V_PAD), F32), pltpu.VMEM((hp, tq, QK_PAD), BF16),
                        pltpu.VMEM((hp, tq, 1), F32)],
    )
    return pl.pallas_call(
        functools.partial(_attn_kernel, tq=tq, wide=wide, past=past),
        grid_spec=grid_spec,
        out_shape=jax.ShapeDtypeStruct((B, L, D_ATT), BF16),
        compiler_params=pltpu.CompilerParams(
            dimension_semantics=("arbitrary", "arbitrary", "arbitrary"),
            vmem_limit_bytes=VMEM_LIMIT_BYTES),
        name="attention",
    )(safe, q, k, v, negb)


def _mlstm_kernel(xcm_ref, vm_ref, gates_ref, og_ref, sgm_ref, o_ref, sza_ref, sga_ref, x_ref,
                  cbuf_ref, c0_ref, n0_ref, m0_ref,
                  bif_ref, cw_ref, cb_ref, wq_ref, wkt_ref, hg_ref, wpm_ref, wpa_ref, wout_ref, tri_ref,
                  y_ref, c_ref, n_ref, m_ref, xpad_ref, hn_ref, *, ch):
    ci = pl.program_id(1)

    @pl.when(ci == 0)
    def _():
        c_ref[...] = c0_ref[...]
        n_ref[...] = n0_ref[...]
        m_ref[...] = m0_ref[...]
        xpad_ref[0:SUBLANES, :] = cbuf_ref[0]

    pa = _dot(o_ref[0] * sza_ref[0], wpa_ref[...])

    xpad_ref[SUBLANES:, :] = xcm_ref[0].astype(F32)

    gpre = gates_ref[0] + bif_ref[...]
    lane = lax.broadcasted_iota(jnp.int32, (ch, LANES), 1)
    lf = jnp.minimum(gpre, 0.0) - jnp.log1p(jnp.exp(-jnp.abs(gpre)))
    lf = jnp.where((lane >= H_M) & (lane < 2 * H_M), lf, 0.0)
    tri = (lax.broadcasted_iota(jnp.int32, (ch, ch), 1)
           <= lax.broadcasted_iota(jnp.int32, (ch, ch), 0))
    bcum = sum(_dot(tri_ref[...], part) for part in _split3(lf))
    xmat = jnp.where(lane < H_M, gpre, -bcum)
    if ch % LANES:
        xmat = jnp.concatenate([xmat, jnp.zeros((LANES - ch % LANES, LANES), F32)], axis=0)
    xrow = xmat.T

    def project(hd):
        sl = slice(hd * DH_M, (hd + 1) * DH_M)
        u = cb_ref[:, sl] + xpad_ref[SUBLANES:, sl] * cw_ref[CONV_W - 1:CONV_W, sl]
        for j in range(CONV_W - 1):
            off = SUBLANES - (CONV_W - 1) + j
            u = u + xpad_ref[off:off + ch, sl] * cw_ref[j:j + 1, sl]
        su_h = (u * _sigmoid(u)).astype(BF16)
        qh = _dot(su_h, wq_ref[hd])
        kt = _dot_nt(wkt_ref[hd], su_h)
        qb = qh.astype(BF16)
        c0 = c_ref[0, hd]
        return qh, kt, _dot(qb, kt.astype(BF16)), _dot(qb, c0.astype(BF16)), c0

    def recur(hd, qh, kt, s_raw, q_c0, c0):
        sl = slice(hd * DH_M, (hd + 1) * DH_M)
        m0 = m_ref[0, hd:hd + 1, 0:1]
        bc = bcum[:, H_M + hd:H_M + hd + 1]
        drow = xrow[hd:hd + 1, :ch] + xrow[H_M + hd:H_M + hd + 1, :ch]
        dlog = jnp.where(tri, bc + drow, NEG)
        g = bc + m0
        m = jnp.maximum(g, jnp.max(dlog, axis=1, keepdims=True))
        w_intra = jnp.exp(dlog - m)
        w_inter = jnp.exp(g - m)

        v = vm_ref[0, :, sl]
        n0 = n_ref[0, hd:hd + 1, :]
        s = s_raw * w_intra
        num = w_inter * q_c0 + _dot(s.astype(BF16), v)
        den = (w_inter * jnp.sum(qh * n0, axis=1, keepdims=True)
               + jnp.sum(s, axis=1, keepdims=True))
        hh = num / jnp.maximum(jnp.abs(den), jnp.exp(-m))
        hn_ref[:, sl] = ((hh * _rms_scale(hh, DH_M)) * hg_ref[:, sl]).astype(BF16)

        m_last = m[ch - 1:ch, :]
        b_last = bc[ch - 1:ch, :]
        w_state = jnp.exp(drow + (b_last - m_last))
        decay = jnp.exp(b_last + m0 - m_last)
        c_ref[0, hd] = decay * c0 + _dot((kt * w_state).astype(BF16), v)
        w8 = jnp.broadcast_to(w_state, (SUBLANES, ch)).astype(BF16)
        n_ref[0, hd:hd + 1, :] = decay * n0 + _dot_nt(w8, kt.astype(BF16))[0:1, :]
        m_ref[0, hd:hd + 1, :] = jnp.broadcast_to(m_last, (1, LANES))

    pending = project(0)
    for hd in range(H_M):
        ahead = project(hd + 1) if hd + 1 < H_M else None
        recur(hd, *pending)
        pending = ahead
    xpad_ref[0:SUBLANES, :] = xpad_ref[ch:ch + SUBLANES, :]

    hm = hn_ref[...] * og_ref[0]
    pm = _dot(hm, wpm_ref[...])
    merged = sgm_ref[0].astype(F32) * pm + sga_ref[0].astype(F32) * pa
    y_ref[0] = x_ref[0] + _dot(merged.astype(BF16), wout_ref[...])


def _mlstm(xcm, vm, gates, og, sgm, o, sza, sga, x, cbuf, c0, n0, m0b, prm, ch):
    B, L, _ = xcm.shape
    seq = lambda w: pl.BlockSpec((1, ch, w), lambda b, c: (b, c, 0))
    per_b = lambda *s: pl.BlockSpec((1,) + s, lambda b, c: (b,) + (0,) * len(s))
    consts = [prm["b_if"], prm["conv_w"], prm["conv_b"], prm["wq_m"], prm["wk_m"], prm["hnorm_g"],
              prm["w_pm"], prm["w_pa"], prm["w_out"],
              jnp.tril(jnp.ones((ch, ch), BF16))]
    return pl.pallas_call(
        functools.partial(_mlstm_kernel, ch=ch),
        grid=(B, L // ch),
        in_specs=[seq(D_MLSTM), seq(D_MLSTM), seq(LANES), seq(D_MLSTM), seq(D_MODEL),
                  seq(D_ATT), seq(D_ATT), seq(D_MODEL), seq(D_MODEL),
                  per_b(SUBLANES, D_MLSTM), per_b(H_M, DH_M, DH_M), per_b(H_M, DH_M),
                  per_b(H_M, LANES)] + [_const_spec(c.shape) for c in consts],
        out_specs=[seq(D_MODEL), per_b(H_M, DH_M, DH_M), per_b(H_M, DH_M), per_b(H_M, LANES)],
        out_shape=[jax.ShapeDtypeStruct((B, L, D_MODEL), F32),
                   jax.ShapeDtypeStruct((B, H_M, DH_M, DH_M), F32),
                   jax.ShapeDtypeStruct((B, H_M, DH_M), F32),
                   jax.ShapeDtypeStruct((B, H_M, LANES), F32)],
        scratch_shapes=[pltpu.VMEM((ch + SUBLANES, D_MLSTM), F32),
                        pltpu.VMEM((ch, D_MLSTM), BF16)],
        compiler_params=pltpu.CompilerParams(
            dimension_semantics=("arbitrary", "arbitrary"), vmem_limit_bytes=VMEM_LIMIT_BYTES),
        name="mlstm",
    )(xcm, vm, gates, og, sgm, o, sza, sga, x, cbuf, c0, n0, m0b, *consts)


def _swap_halves(a):
    half = ROPE // 2
    return jnp.concatenate([a[..., half:], a[..., :half]], axis=-1)


def _prepare_params(norm_g, w_in, b_if, conv_w, conv_b, wq_m, wk_m, hnorm_g, qn_g, w_uq, kvn_g,
                    w_ukv, g_qn, g_qr, g_kn, g_kr, w_pm, w_pa, w_out):
    cols, start = [], 0
    for w in SPLITS:
        cols.append(w_in[:, start:start + w])
        start += w
    xcm, vm, ig, fg, og, zm, cq, ckv, kr, za, gm, ga = cols
    pad = jnp.zeros((D_MODEL, LANES - 2 * H_M), F32)
    w_small = jnp.concatenate([ckv, kr, _swap_halves(kr), ig, fg, pad], axis=1)
    wuq = w_uq.reshape(Q_LORA, H_A, QK_DIM)
    wuq = jnp.concatenate([wuq, _swap_halves(wuq[..., NOPE:])], axis=-1)
    row = lambda a: a.reshape(1, -1).astype(F32)
    pair = lambda g: row(jnp.concatenate([g, _swap_halves(g)]))
    pair2 = lambda g: row(jnp.concatenate([_swap_halves(g), g]))
    return {
        "norm_g": row(norm_g),
        "w_big": jnp.concatenate([xcm, vm, og, zm, za, gm, ga], axis=1).astype(BF16),
        "w_cq": cq.astype(BF16),
        "w_small": w_small.astype(BF16),
        "qn_g": row(qn_g),
        "w_uq": wuq.reshape(Q_LORA, H_A * 2 * LANES).astype(BF16),
        "kvn_g": row(kvn_g),
        "g_kr": pair(g_kr), "g_kr2": pair2(g_kr),
        "g_qn": row(g_qn),
        "g_qr": pair(g_qr), "g_qr2": pair2(g_qr),
        "w_ukv": w_ukv.astype(BF16),
        "g_kn": row(g_kn),
        "q_bound2": (ATT_SCALE * LOG2E) ** 2 * (NOPE * jnp.max(g_qn * g_qn)
                                                + ROPE * jnp.max(g_qr * g_qr)),
        "kn_bound2": NOPE * jnp.max(g_kn * g_kn),
        "b_if": row(jnp.concatenate([b_if, jnp.zeros((LANES - 2 * H_M,), F32)])),
        "conv_w": conv_w, "conv_b": row(conv_b),
        "wq_m": wq_m.astype(BF16),
        "wk_m": (jnp.swapaxes(wk_m, 1, 2) * (DH_M ** -0.5)).astype(BF16),
        "hnorm_g": row(hnorm_g),
        "w_pm": w_pm.astype(BF16), "w_pa": w_pa.astype(BF16), "w_out": w_out.astype(BF16),
    }


def _rope_tables(past, length):
    half = ROPE // 2
    inv = jnp.power(ROPE_THETA, -jnp.arange(half, dtype=F32) / half)
    ang = (past + jnp.arange(length)).astype(F32)[:, None] * inv[None, :]
    cos, sin = jnp.cos(ang), jnp.sin(ang)
    return jnp.tile(cos, (1, 4)), jnp.tile(jnp.concatenate([-sin, sin], axis=1), (1, 2))


def _hybrid_layer(x, past_ckv, past_kr, conv_buf, c0, n0, m0, prm, *, tm, tq, wide, tkv, ch):
    B, L, _ = x.shape
    past = past_ckv.shape[1]
    cos_t, sin_t = _rope_tables(past, L)
    (xcm, vm, og, sza, sgm, sga, gates, ckv_new, kr_new, q, tail) = _inproj(x, prm, cos_t, sin_t, tm)
    seq = lambda a: a.reshape(B, L, a.shape[-1])
    ckv_new, kr_new = seq(ckv_new), seq(kr_new)

    ckv_all = jnp.concatenate([past_ckv, ckv_new], axis=1) if past else ckv_new
    kr_all = jnp.concatenate([past_kr, kr_new], axis=1) if past else kr_new
    T = kr_all.shape[1]
    aug = jnp.concatenate([jnp.ones((B, T, 1), F32), jnp.zeros((B, T, LANES - ROPE - 1), F32)], -1)
    k, v, kmax_tiles = _kvup(ckv_all, jnp.concatenate([kr_all, aug], axis=-1), prm, tkv)
    kmax2 = jnp.max(kmax_tiles[:, :, 0, 0], axis=1) + prm["kn_bound2"]
    o = _attention(q, k, v, jnp.broadcast_to(kmax2[:, None], (B, H_A)), prm["q_bound2"], tq,
                   wide)

    cbuf = jnp.concatenate(
        [jnp.zeros((B, SUBLANES - (CONV_W - 1), D_MLSTM), F32), conv_buf.astype(F32)], axis=1)
    m0b = jnp.broadcast_to(m0.astype(F32)[:, :, None], (B, H_M, LANES))
    y, c1, n1, m1 = _mlstm(seq(xcm), seq(vm), seq(gates), seq(og), seq(sgm), o, seq(sza), seq(sga),
                           x, cbuf, c0.astype(F32), n0.astype(F32), m0b, prm, ch)
    conv_new = tail[:, SUBLANES - (CONV_W - 1):, :]
    return y, ckv_new, kr_new, conv_new, c1, n1, m1[:, :, 0]


def kernel(x_prompt, x_sample, cache_ckv, cache_kr, state_conv, state_C, state_n, state_m,
           norm_g, w_in, b_if, conv_w, conv_b, wq_m, wk_m, hnorm_g,
           qn_g, w_uq, kvn_g, w_ukv, g_qn, g_qr, g_kn, g_kr, w_pm, w_pa, w_out):
    dt = x_prompt.dtype
    B = x_prompt.shape[0]
    depth = norm_g.shape[0]
    yp, ys = x_prompt, x_sample
    outs_p = [[] for _ in range(6)]
    outs_s = [[] for _ in range(6)]
    for l in range(depth):
        prm = _prepare_params(norm_g[l], w_in[l], b_if[l], conv_w[l], conv_b[l], wq_m[l], wk_m[l],
                              hnorm_g[l], qn_g[l], w_uq[l], kvn_g[l], w_ukv[l], g_qn[l], g_qr[l],
                              g_kn[l], g_kr[l], w_pm[l], w_pa[l], w_out[l])
        yp, *res_p = _hybrid_layer(
            yp, jnp.zeros((B, 0, KV_LORA), dt), jnp.zeros((B, 0, ROPE), dt),
            jnp.zeros((B, CONV_W - 1, D_MLSTM), dt), jnp.zeros((B, H_M, DH_M, DH_M), F32),
            jnp.zeros((B, H_M, DH_M), F32), jnp.zeros((B, H_M), F32), prm,
            tm=512, tq=512, wide=1024, tkv=512, ch=256)
        Ls = x_sample.shape[1]
        ys, *res_s = _hybrid_layer(
            ys, cache_ckv[l], cache_kr[l], state_conv[l], state_C[l], state_n[l], state_m[l], prm,
            tm=Ls, tq=Ls, wide=cache_ckv.shape[2], tkv=cache_ckv.shape[2] + Ls, ch=Ls)
        for lst, a in zip(outs_p, res_p):
            lst.append(a.astype(dt))
        for lst, a in zip(outs_s, res_s):
            lst.append(a.astype(dt))
    return (yp, ys) + tuple(jnp.stack(a) for a in outs_p) + tuple(jnp.stack(a) for a in outs_s)
```

```python
import functools
import math

import jax
import jax.numpy as jnp
from jax import lax
from jax.experimental import pallas as pl
from jax.experimental.pallas import tpu as pltpu

F32 = jnp.float32
BF16 = jnp.bfloat16

D_MODEL = 1024
CHUNK = 64
EPS = 1e-6
NEG = -1e30
D_MLSTM = D_MODEL
H_M = 4
DH_M = D_MLSTM // H_M
CONV_W = 4
H_A = 8
NOPE = 128
ROPE = 64
V_DIM = 128
Q_LORA = 384
KV_LORA = 256
D_ATT = H_A * V_DIM
ROPE_THETA = 10000.0
ATT_SCALE = (NOPE + ROPE) ** -0.5
LOG2E = math.log2(math.e)
QK_DIM = NOPE + ROPE
QK_PAD = 256
V_PAD = 256
SAFE_BOUND = 50.0
SPLITS = (D_MLSTM, D_MLSTM, H_M, H_M, D_MLSTM, D_MLSTM, Q_LORA, KV_LORA, ROPE, D_ATT, D_MODEL, D_MODEL)

LANES = 128
SUBLANES = 8
VMEM_LIMIT_BYTES = 56 * 1024 * 1024


def _const_spec(shape):
    nd = len(shape)
    return pl.BlockSpec(shape, lambda *_: (0,) * nd, pipeline_mode=pl.Buffered(1))


def _dot(a, b):
    return jnp.dot(a, b, preferred_element_type=F32)


def _dot_nt(a, b):
    return lax.dot_general(a, b, (((1,), (1,)), ((), ())), preferred_element_type=F32)


def _rms_scale(x, width):
    return lax.rsqrt(jnp.sum(x * x, axis=-1, keepdims=True) * (1.0 / width) + EPS)


def _sigmoid(x):
    return 1.0 / (1.0 + jnp.exp(-x))


def _split3(x):
    a = x.astype(BF16)
    r = x - a.astype(F32)
    b = r.astype(BF16)
    c = (r - b.astype(F32)).astype(BF16)
    return a, b, c


def _rope_half_norm(t, gain, gain_rolled, cos_t, sin_t):
    lane = lax.broadcasted_iota(jnp.int32, t.shape, 1)
    sq = jnp.where(lane < ROPE, t * t, 0.0)
    r = lax.rsqrt(jnp.sum(sq, axis=-1, keepdims=True) * (1.0 / ROPE) + EPS)
    t_sw = pltpu.roll(t, ROPE, axis=1)
    return r * (t * gain * cos_t + t_sw * gain_rolled * sin_t)


def _inproj_kernel(*refs, emit_kv):
    (x_ref, ng_ref, wbig_ref, wcq_ref, wsm_ref, qng_ref, wuq_ref, kvng_ref, gkr_ref, gkr2_ref,
     gqn_ref, gqr_ref, gqr2_ref, wukv_ref, gkn_ref, cos_ref, sin_ref, negb_ref,
     xcm_ref, vm_ref, og_ref, sza_ref, sgm_ref, sga_ref, gates_ref, ckv_ref, kr_ref, q_ref,
     tail_ref, *kv_refs) = refs
    x = x_ref[...]
    h = ((x * _rms_scale(x, D_MODEL)) * ng_ref[...]).astype(BF16)
    cos_t = cos_ref[...]
    sin_t = sin_ref[...]
    lane = lax.broadcasted_iota(jnp.int32, (x.shape[0], LANES), 1)

    def seg(i):
        return _dot(h, wbig_ref[:, i * D_MODEL:(i + 1) * D_MODEL])

    cq = _dot(h, wcq_ref[...])
    cqn = ((cq * _rms_scale(cq, Q_LORA)) * qng_ref[...]).astype(BF16)
    qscale = ATT_SCALE * LOG2E
    q_aug = jnp.where(lane == ROPE, negb_ref[0], 0.0)

    def q_head(hd):
        qh = _dot(cqn, wuq_ref[:, hd * 2 * LANES:(hd + 1) * 2 * LANES])
        qn = qh[:, :NOPE]
        qn = (qn * _rms_scale(qn, NOPE)) * gqn_ref[...]
        q_ref[0, hd, :, :NOPE] = (qn * qscale).astype(BF16)
        qr = _rope_half_norm(qh[:, NOPE:], gqr_ref[...], gqr2_ref[...], cos_t, sin_t)
        q_ref[0, hd, :, NOPE:] = jnp.where(lane < ROPE, qr * qscale, q_aug).astype(BF16)

    t = _dot(h, wsm_ref[...])
    c = t[:, :KV_LORA]
    ckv = (c * _rms_scale(c, KV_LORA)) * kvng_ref[...]
    ckv_ref[...] = ckv
    kr = _rope_half_norm(t[:, KV_LORA:KV_LORA + LANES], gkr_ref[...], gkr2_ref[...], cos_t, sin_t)
    kr_ref[...] = kr[:, :ROPE]
    gates_ref[...] = t[:, KV_LORA + LANES:]

    if emit_kv:
        k_ref, v_ref = kv_refs
        ckv_b = ckv.astype(BF16)
        kr_aug = jnp.where(lane < ROPE, kr, (lane == ROPE).astype(F32)).astype(BF16)

        def kv_head(hd):
            _kv_head(hd, ckv_b, kr_aug, wukv_ref, gkn_ref, k_ref, v_ref)
    else:
        def kv_head(hd):
            pass

    q_head(0)
    kv_head(0)
    z = seg(3)
    q_head(1)
    kv_head(1)
    og_ref[...] = (_sigmoid(seg(2)) * (z * _sigmoid(z))).astype(BF16)
    q_head(2)
    kv_head(2)
    z = seg(4)
    sza_ref[...] = (z * _sigmoid(z)).astype(BF16)
    q_head(3)
    kv_head(3)
    sgm_ref[...] = _sigmoid(seg(5)).astype(BF16)
    q_head(4)
    kv_head(4)
    sga_ref[...] = _sigmoid(seg(6)).astype(BF16)
    q_head(5)
    kv_head(5)
    q_head(6)
    kv_head(6)
    q_head(7)
    kv_head(7)
    xcm = seg(0)
    xcm_ref[...] = xcm.astype(BF16)
    tail_ref[0] = xcm[xcm.shape[0] - SUBLANES:, :]
    vm_ref[...] = seg(1).astype(BF16)


def _kv_head(hd, ckv_b, kr_aug, w_ref, gkn_ref, k_ref, v_ref):
    kv = _dot(ckv_b, w_ref[:, hd * 2 * LANES:(hd + 1) * 2 * LANES])
    kn = kv[:, :NOPE]
    k_ref[0, hd, :, :NOPE] = ((kn * _rms_scale(kn, NOPE)) * gkn_ref[...]).astype(BF16)
    k_ref[0, hd, :, NOPE:] = kr_aug
    lane = lax.broadcasted_iota(jnp.int32, kv.shape, 1)
    v_aug = jnp.where(lane < V_DIM, pltpu.roll(kv, V_DIM, axis=1), (lane == V_DIM).astype(F32))
    v_ref[0, hd] = v_aug.astype(BF16)


def _inproj(x, prm, cos_t, sin_t, negb, tm, emit_kv):
    B, L, _ = x.shape
    T = B * L
    tps = L // tm
    x2 = x.reshape(T, D_MODEL)
    row = lambda w: pl.BlockSpec((tm, w), lambda i: (i, 0))
    pos = pl.BlockSpec((tm, LANES), lambda i: (i % tps, 0))
    head_rows = lambda w: pl.BlockSpec((1, H_A, tm, w), lambda i: (i // tps, 0, i % tps, 0))
    consts = [prm["norm_g"], prm["w_big"], prm["w_cq"], prm["w_small"], prm["qn_g"], prm["w_uq"],
              prm["kvn_g"], prm["g_kr"], prm["g_kr2"], prm["g_qn"], prm["g_qr"], prm["g_qr2"],
              prm["w_ukv"], prm["g_kn"]]
    wide = jax.ShapeDtypeStruct((T, D_MODEL), BF16)
    heads = lambda w: jax.ShapeDtypeStruct((B, H_A, L, w), BF16)
    out_shape = [wide] * 6 + [
        jax.ShapeDtypeStruct((T, LANES), F32),
        jax.ShapeDtypeStruct((T, KV_LORA), F32),
        jax.ShapeDtypeStruct((T, ROPE), F32),
        heads(QK_PAD),
        jax.ShapeDtypeStruct((B, SUBLANES, D_MLSTM), F32),
    ] + ([heads(QK_PAD), heads(V_PAD)] if emit_kv else [])
    out_specs = [row(D_MODEL)] * 6 + [
        row(LANES), row(KV_LORA), row(ROPE), head_rows(QK_PAD),
        pl.BlockSpec((1, SUBLANES, D_MLSTM), lambda i: (i // tps, 0, 0)),
    ] + ([head_rows(QK_PAD), head_rows(V_PAD)] if emit_kv else [])
    return pl.pallas_call(
        functools.partial(_inproj_kernel, emit_kv=emit_kv),
        grid=(T // tm,),
        in_specs=[row(D_MODEL)] + [_const_spec(c.shape) for c in consts] + [
            pos, pos, pl.BlockSpec((1, 1, LANES), lambda i: (i // tps, 0, 0))],
        out_specs=out_specs,
        out_shape=out_shape,
        compiler_params=pltpu.CompilerParams(
            dimension_semantics=("arbitrary",), vmem_limit_bytes=VMEM_LIMIT_BYTES),
        name="inproj",
    )(x2, *consts, cos_t, sin_t, negb)


def _kvup_kernel(ckv_ref, kr_ref, w_ref, gkn_ref, k_ref, v_ref):
    ckv_b = ckv_ref[0].astype(BF16)
    kr_aug = kr_ref[0].astype(BF16)
    for hd in range(H_A):
        _kv_head(hd, ckv_b, kr_aug, w_ref, gkn_ref, k_ref, v_ref)


def _kvup(ckv, kr, prm, tm):
    B, T, _ = ckv.shape
    return pl.pallas_call(
        _kvup_kernel,
        grid=(B, T // tm),
        in_specs=[pl.BlockSpec((1, tm, KV_LORA), lambda b, i: (b, i, 0)),
                  pl.BlockSpec((1, tm, LANES), lambda b, i: (b, i, 0)),
                  _const_spec(prm["w_ukv"].shape), _const_spec(prm["g_kn"].shape)],
        out_specs=[pl.BlockSpec((1, H_A, tm, QK_PAD), lambda b, i: (b, 0, i, 0)),
                   pl.BlockSpec((1, H_A, tm, V_PAD), lambda b, i: (b, 0, i, 0))],
        out_shape=[jax.ShapeDtypeStruct((B, H_A, T, QK_PAD), BF16),
                   jax.ShapeDtypeStruct((B, H_A, T, V_PAD), BF16)],
        compiler_params=pltpu.CompilerParams(
            dimension_semantics=("arbitrary", "arbitrary"), vmem_limit_bytes=VMEM_LIMIT_BYTES),
        name="kvup",
    )(ckv, kr, prm["w_ukv"], prm["g_kn"])


HEADS_PER_STEP = 2


def _attn_kernel(safe_ref, q_ref, k_ref, v_ref, o_ref, acc_ref, m_ref, *, tq, wide, past):
    b, qi = pl.program_id(0), pl.program_id(2)
    diag0 = past + qi * tq
    heads = range(HEADS_PER_STEP)

    def chunk_mask(width):
        rows = lax.broadcasted_iota(jnp.int32, (tq, width), 0) // CHUNK
        cols = lax.broadcasted_iota(jnp.int32, (tq, width), 1) // CHUNK
        return cols <= rows + (width - tq) // CHUNK

    acc_ref[...] = jnp.zeros_like(acc_ref)

    @pl.when(safe_ref[b] == 1)
    def _bounded():
        def step(ks, size, msk):
            scores = [_dot_nt(q_ref[0, h], k_ref[0, h, pl.ds(ks, size), :]) for h in heads]
            for h in heads:
                s = scores[h] if msk is None else jnp.where(msk, scores[h], NEG)
                p = jnp.exp2(s).astype(BF16)
                acc_ref[h] += _dot(p, v_ref[0, h, pl.ds(ks, size), :])

        @pl.loop(0, diag0 // wide)
        def _(j):
            step(pl.multiple_of(j * wide, wide), wide, None)

        @pl.when(diag0 % wide == 0)
        def _():
            step(pl.multiple_of(diag0, tq), tq, chunk_mask(tq))

        @pl.when(diag0 % wide != 0)
        def _():
            step(pl.multiple_of(diag0 - tq, tq), 2 * tq, chunk_mask(2 * tq))

    @pl.when(safe_ref[b] == 0)
    def _online():
        m_ref[...] = jnp.full_like(m_ref, NEG)

        def step(ks, msk):
            for h in heads:
                s = _dot_nt(q_ref[0, h], k_ref[0, h, pl.ds(ks, tq), :])
                if msk is not None:
                    s = jnp.where(msk, s, NEG)
                m_old = m_ref[h]
                m_new = jnp.maximum(m_old, jnp.max(s, axis=1, keepdims=True))
                p = jnp.exp2(s - m_new).astype(BF16)
                acc_ref[h] = (jnp.exp2(m_old - m_new) * acc_ref[h]
                              + _dot(p, v_ref[0, h, pl.ds(ks, tq), :]))
                m_ref[h] = m_new

        @pl.loop(0, diag0 // tq)
        def _(j):
            step(pl.multiple_of(j * tq, tq), None)

        step(pl.multiple_of(diag0, tq), chunk_mask(tq))

    for h in heads:
        acc = acc_ref[h]
        o_ref[0, :, h * V_DIM:(h + 1) * V_DIM] = (
            acc[:, :V_DIM] / acc[:, V_DIM:V_DIM + 1]).astype(BF16)


def _attention(q, k, v, safe, tq, wide):
    B, H, L, _ = q.shape
    T = k.shape[2]
    past = T - L
    hp = HEADS_PER_STEP
    assert past % wide == 0 and (wide == 2 * tq or L == tq) and L % tq == 0 and H % hp == 0
    grid_spec = pltpu.PrefetchScalarGridSpec(
        num_scalar_prefetch=1,
        grid=(B, H // hp, L // tq),
        in_specs=[pl.BlockSpec((1, hp, tq, QK_PAD), lambda b, h, i, s: (b, h, i, 0)),
                  pl.BlockSpec((1, hp, T, QK_PAD), lambda b, h, i, s: (b, h, 0, 0)),
                  pl.BlockSpec((1, hp, T, V_PAD), lambda b, h, i, s: (b, h, 0, 0))],
        out_specs=pl.BlockSpec((1, tq, hp * V_DIM), lambda b, h, i, s: (b, i, h)),
        scratch_shapes=[pltpu.VMEM((hp, tq, V_PAD), F32), pltpu.VMEM((hp, tq, 1), F32)],
    )
    return pl.pallas_call(
        functools.partial(_attn_kernel, tq=tq, wide=wide, past=past),
        grid_spec=grid_spec,
        out_shape=jax.ShapeDtypeStruct((B, L, D_ATT), BF16),
        compiler_params=pltpu.CompilerParams(
            dimension_semantics=("arbitrary", "arbitrary", "arbitrary"),
            vmem_limit_bytes=VMEM_LIMIT_BYTES),
        name="attention",
    )(safe, q, k, v)


def _mlstm_kernel(xcm_ref, vm_ref, gates_ref, og_ref, sgm_ref, o_ref, sza_ref, sga_ref, x_ref,
                  cbuf_ref, c0_ref, n0_ref, m0_ref, bif_ref, cw_ref, cb_ref, wq_ref, wkt_ref,
                  hg_ref, wpm_ref, wpa_ref, wout_ref, tri_ref,
                  y_ref, c_ref, n_ref, m_ref, xpad_ref, hn_ref, *, ch):
    ci = pl.program_id(1)

    @pl.when(ci == 0)
    def _():
        c_ref[...] = c0_ref[...]
        n_ref[...] = n0_ref[...]
        m_ref[...] = m0_ref[...]
        xpad_ref[0:SUBLANES, :] = cbuf_ref[0]

    pa = _dot(o_ref[0] * sza_ref[0], wpa_ref[...])

    xpad_ref[SUBLANES:, :] = xcm_ref[0].astype(F32)

    gpre = gates_ref[0] + bif_ref[...]
    lane = lax.broadcasted_iota(jnp.int32, (ch, LANES), 1)
    lf = jnp.minimum(gpre, 0.0) - jnp.log1p(jnp.exp(-jnp.abs(gpre)))
    lf = jnp.where((lane >= H_M) & (lane < 2 * H_M), lf, 0.0)
    tri = (lax.broadcasted_iota(jnp.int32, (ch, ch), 1)
           <= lax.broadcasted_iota(jnp.int32, (ch, ch), 0))
    bcum = sum(_dot(tri_ref[...], part) for part in _split3(lf))
    xmat = jnp.where(lane < H_M, gpre, -bcum)
    if ch % LANES:
        xmat = jnp.concatenate([xmat, jnp.zeros((LANES - ch % LANES, LANES), F32)], axis=0)
    xrow = xmat.T

    def project(hd):
        sl = slice(hd * DH_M, (hd + 1) * DH_M)
        u = cb_ref[:, sl] + xpad_ref[SUBLANES:, sl] * cw_ref[CONV_W - 1:CONV_W, sl]
        for j in range(CONV_W - 1):
            off = SUBLANES - (CONV_W - 1) + j
            u = u + xpad_ref[off:off + ch, sl] * cw_ref[j:j + 1, sl]
        su_h = (u * _sigmoid(u)).astype(BF16)
        qh = _dot(su_h, wq_ref[hd])
        kt = _dot_nt(wkt_ref[hd], su_h)
        qb = qh.astype(BF16)
        c0 = c_ref[0, hd]
        return qh, kt, _dot(qb, kt.astype(BF16)), _dot(qb, c0.astype(BF16)), c0

    def recur(hd, qh, kt, s_raw, q_c0, c0):
        sl = slice(hd * DH_M, (hd + 1) * DH_M)
        m0 = m_ref[0, hd:hd + 1, 0:1]
        bc = bcum[:, H_M + hd:H_M + hd + 1]
        drow = xrow[hd:hd + 1, :ch] + xrow[H_M + hd:H_M + hd + 1, :ch]
        dlog = jnp.where(tri, bc + drow, NEG)
        g = bc + m0
        m = jnp.maximum(g, jnp.max(dlog, axis=1, keepdims=True))
        w_intra = jnp.exp(dlog - m)
        w_inter = jnp.exp(g - m)

        v = vm_ref[0, :, sl]
        n0 = n_ref[0, hd:hd + 1, :]
        s = s_raw * w_intra
        num = w_inter * q_c0 + _dot(s.astype(BF16), v)
        den = (w_inter * jnp.sum(qh * n0, axis=1, keepdims=True)
               + jnp.sum(s, axis=1, keepdims=True))
        hh = num / jnp.maximum(jnp.abs(den), jnp.exp(-m))
        hn_ref[:, sl] = ((hh * _rms_scale(hh, DH_M)) * hg_ref[:, sl]).astype(BF16)

        m_last = m[ch - 1:ch, :]
        b_last = bc[ch - 1:ch, :]
        w_state = jnp.exp(drow + (b_last - m_last))
        decay = jnp.exp(b_last + m0 - m_last)
        c_ref[0, hd] = decay * c0 + _dot((kt * w_state).astype(BF16), v)
        w8 = jnp.broadcast_to(w_state, (SUBLANES, ch)).astype(BF16)
        n_ref[0, hd:hd + 1, :] = decay * n0 + _dot_nt(w8, kt.astype(BF16))[0:1, :]
        m_ref[0, hd:hd + 1, :] = jnp.broadcast_to(m_last, (1, LANES))

    pending = project(0)
    for hd in range(H_M):
        ahead = project(hd + 1) if hd + 1 < H_M else None
        recur(hd, *pending)
        pending = ahead
    xpad_ref[0:SUBLANES, :] = xpad_ref[ch:ch + SUBLANES, :]

    hm = hn_ref[...] * og_ref[0]
    pm = _dot(hm, wpm_ref[...])
    merged = sgm_ref[0] * pm.astype(BF16) + sga_ref[0] * pa.astype(BF16)
    y_ref[0] = x_ref[0] + _dot(merged, wout_ref[...])


def _mlstm(xcm, vm, gates, og, sgm, o, sza, sga, x, cbuf, c0, n0, m0b, prm, ch):
    B, L, _ = xcm.shape
    seq = lambda w: pl.BlockSpec((1, ch, w), lambda b, c: (b, c, 0))
    per_b = lambda *s: pl.BlockSpec((1,) + s, lambda b, c: (b,) + (0,) * len(s))
    consts = [prm["b_if"], prm["conv_w"], prm["conv_b"], prm["wq_m"], prm["wk_m"], prm["hnorm_g"],
              prm["w_pm"], prm["w_pa"], prm["w_out"],
              jnp.tril(jnp.ones((ch, ch), BF16))]
    return pl.pallas_call(
        functools.partial(_mlstm_kernel, ch=ch),
        grid=(B, L // ch),
        in_specs=[seq(D_MLSTM), seq(D_MLSTM), seq(LANES), seq(D_MLSTM), seq(D_MODEL),
                  seq(D_ATT), seq(D_ATT), seq(D_MODEL), seq(D_MODEL),
                  per_b(SUBLANES, D_MLSTM), per_b(H_M, DH_M, DH_M), per_b(H_M, DH_M),
                  per_b(H_M, LANES)] + [_const_spec(c.shape) for c in consts],
        out_specs=[seq(D_MODEL), per_b(H_M, DH_M, DH_M), per_b(H_M, DH_M), per_b(H_M, LANES)],
        out_shape=[jax.ShapeDtypeStruct((B, L, D_MODEL), F32),
                   jax.ShapeDtypeStruct((B, H_M, DH_M, DH_M), F32),
                   jax.ShapeDtypeStruct((B, H_M, DH_M), F32),
                   jax.ShapeDtypeStruct((B, H_M, LANES), F32)],
        scratch_shapes=[pltpu.VMEM((ch + SUBLANES, D_MLSTM), F32),
                        pltpu.VMEM((ch, D_MLSTM), BF16)],
        compiler_params=pltpu.CompilerParams(
            dimension_semantics=("arbitrary", "arbitrary"), vmem_limit_bytes=VMEM_LIMIT_BYTES),
        name="mlstm",
    )(xcm, vm, gates, og, sgm, o, sza, sga, x, cbuf, c0, n0, m0b, *consts)


def _swap_halves(a):
    half = ROPE // 2
    return jnp.concatenate([a[..., half:], a[..., :half]], axis=-1)


def _prepare_params(norm_g, w_in, b_if, conv_w, conv_b, wq_m, wk_m, hnorm_g, qn_g, w_uq, kvn_g,
                    w_ukv, g_qn, g_qr, g_kn, g_kr, w_pm, w_pa, w_out):
    cols, start = [], 0
    for w in SPLITS:
        cols.append(w_in[:, start:start + w])
        start += w
    xcm, vm, ig, fg, og, zm, cq, ckv, kr, za, gm, ga = cols
    pad = jnp.zeros((D_MODEL, LANES - 2 * H_M), F32)
    w_small = jnp.concatenate([ckv, kr, _swap_halves(kr), ig, fg, pad], axis=1)
    wuq = w_uq.reshape(Q_LORA, H_A, QK_DIM)
    wuq = jnp.concatenate([wuq, _swap_halves(wuq[..., NOPE:])], axis=-1)
    row = lambda a: a.reshape(1, -1).astype(F32)
    pair = lambda g: row(jnp.concatenate([g, _swap_halves(g)]))
    pair2 = lambda g: row(jnp.concatenate([_swap_halves(g), g]))
    return {
        "norm_g": row(norm_g),
        "w_big": jnp.concatenate([xcm, vm, og, zm, za, gm, ga], axis=1).astype(BF16),
        "w_cq": cq.astype(BF16),
        "w_small": w_small.astype(BF16),
        "qn_g": row(qn_g),
        "w_uq": wuq.reshape(Q_LORA, H_A * 2 * LANES).astype(BF16),
        "kvn_g": row(kvn_g),
        "g_kr": pair(g_kr), "g_kr2": pair2(g_kr),
        "g_qn": row(g_qn),
        "g_qr": pair(g_qr), "g_qr2": pair2(g_qr),
        "w_ukv": w_ukv.astype(BF16),
        "g_kn": row(g_kn),
        "q_bound2": (ATT_SCALE * LOG2E) ** 2 * (NOPE * jnp.max(g_qn * g_qn)
                                                + ROPE * jnp.max(g_qr * g_qr)),
        "kn_bound2": NOPE * jnp.max(g_kn * g_kn),
        "kr_bound2": ROPE * jnp.max(g_kr * g_kr),
        "b_if": row(jnp.concatenate([b_if, jnp.zeros((LANES - 2 * H_M,), F32)])),
        "conv_w": conv_w, "conv_b": row(conv_b),
        "wq_m": wq_m.astype(BF16),
        "wk_m": (jnp.swapaxes(wk_m, 1, 2) * (DH_M ** -0.5)).astype(BF16),
        "hnorm_g": row(hnorm_g),
        "w_pm": w_pm.astype(BF16), "w_pa": w_pa.astype(BF16), "w_out": w_out.astype(BF16),
    }


def _rope_tables(past, length):
    half = ROPE // 2
    inv = jnp.power(ROPE_THETA, -jnp.arange(half, dtype=F32) / half)
    ang = (past + jnp.arange(length)).astype(F32)[:, None] * inv[None, :]
    cos, sin = jnp.cos(ang), jnp.sin(ang)
    return jnp.tile(cos, (1, 4)), jnp.tile(jnp.concatenate([-sin, sin], axis=1), (1, 2))


def _hybrid_layer(x, past_ckv, past_kr, conv_buf, c0, n0, m0, prm, *, tm, tq, wide, ch):
    B, L, _ = x.shape
    past = past_ckv.shape[1]
    cos_t, sin_t = _rope_tables(past, L)
    seq = lambda a: a.reshape(B, L, a.shape[-1])

    kr2 = jnp.broadcast_to(prm["kr_bound2"], (B,))
    if past:
        pk = past_kr.astype(F32)
        kr2 = jnp.maximum(kr2, jnp.max(jnp.sum(pk * pk, axis=-1), axis=1))
    bound = jnp.sqrt((prm["kn_bound2"] + kr2 + 1.0) * prm["q_bound2"]) * 1.01
    safe = bound <= SAFE_BOUND
    negb = jnp.broadcast_to(jnp.where(safe, -bound, 0.0)[:, None, None], (B, 1, LANES))

    outs = _inproj(x, prm, cos_t, sin_t, negb, tm, emit_kv=not past)
    xcm, vm, og, sza, sgm, sga, gates, ckv_new, kr_new, q, tail = outs[:11]
    ckv_new, kr_new = seq(ckv_new), seq(kr_new)
    if past:
        T = past + L
        aug = jnp.concatenate(
            [jnp.ones((B, T, 1), F32), jnp.zeros((B, T, LANES - ROPE - 1), F32)], axis=-1)
        kr_all = jnp.concatenate([jnp.concatenate([past_kr, kr_new], axis=1), aug], axis=-1)
        k, v = _kvup(jnp.concatenate([past_ckv, ckv_new], axis=1), kr_all, prm, T)
    else:
        k, v = outs[11:]
    o = _attention(q, k, v, safe.astype(jnp.int32), tq, wide)

    cbuf = jnp.concatenate(
        [jnp.zeros((B, SUBLANES - (CONV_W - 1), D_MLSTM), F32), conv_buf.astype(F32)], axis=1)
    m0b = jnp.broadcast_to(m0.astype(F32)[:, :, None], (B, H_M, LANES))
    y, c1, n1, m1 = _mlstm(seq(xcm), seq(vm), seq(gates), seq(og), seq(sgm), o, seq(sza), seq(sga),
                           x, cbuf, c0.astype(F32), n0.astype(F32), m0b, prm, ch)
    conv_new = tail[:, SUBLANES - (CONV_W - 1):, :]
    return y, ckv_new, kr_new, conv_new, c1, n1, m1[:, :, 0]


def kernel(x_prompt, x_sample, cache_ckv, cache_kr, state_conv, state_C, state_n, state_m,
           norm_g, w_in, b_if, conv_w, conv_b, wq_m, wk_m, hnorm_g,
           qn_g, w_uq, kvn_g, w_ukv, g_qn, g_qr, g_kn, g_kr, w_pm, w_pa, w_out):
    dt = x_prompt.dtype
    B = x_prompt.shape[0]
    depth = norm_g.shape[0]
    yp, ys = x_prompt, x_sample
    outs_p = [[] for _ in range(6)]
    outs_s = [[] for _ in range(6)]
    for l in range(depth):
        prm = _prepare_params(norm_g[l], w_in[l], b_if[l], conv_w[l], conv_b[l], wq_m[l], wk_m[l],
                              hnorm_g[l], qn_g[l], w_uq[l], kvn_g[l], w_ukv[l], g_qn[l], g_qr[l],
                              g_kn[l], g_kr[l], w_pm[l], w_pa[l], w_out[l])
        yp, *res_p = _hybrid_layer(
            yp, jnp.zeros((B, 0, KV_LORA), dt), jnp.zeros((B, 0, ROPE), dt),
            jnp.zeros((B, CONV_W - 1, D_MLSTM), dt), jnp.zeros((B, H_M, DH_M, DH_M), F32),
            jnp.zeros((B, H_M, DH_M), F32), jnp.zeros((B, H_M), F32), prm,
            tm=512, tq=512, wide=1024, ch=256)
        Ls = x_sample.shape[1]
        ys, *res_s = _hybrid_layer(
            ys, cache_ckv[l], cache_kr[l], state_conv[l], state_C[l], state_n[l], state_m[l], prm,
            tm=Ls, tq=Ls, wide=cache_ckv.shape[2], ch=Ls)
        for lst, a in zip(outs_p, res_p):
            lst.append(a.astype(dt))
        for lst, a in zip(outs_s, res_s):
            lst.append(a.astype(dt))
    return (yp, ys) + tuple(jnp.stack(a) for a in outs_p) + tuple(jnp.stack(a) for a in outs_s)
```

```python
import functools
import math

import jax
import jax.numpy as jnp
from jax import lax
from jax.experimental import pallas as pl
from jax.experimental.pallas import tpu as pltpu

F32 = jnp.float32
BF16 = jnp.bfloat16

D_MODEL = 1024
CHUNK = 64
EPS = 1e-6
NEG = -1e30
D_MLSTM = D_MODEL
H_M = 4
DH_M = D_MLSTM // H_M
CONV_W = 4
H_A = 8
NOPE = 128
ROPE = 64
V_DIM = 128
Q_LORA = 384
KV_LORA = 256
D_ATT = H_A * V_DIM
ROPE_THETA = 10000.0
ATT_SCALE = (NOPE + ROPE) ** -0.5
LOG2E = math.log2(math.e)
QK_DIM = NOPE + ROPE
QK_PAD = 256
V_PAD = 256
SAFE_BOUND = 50.0
SPLITS = (D_MLSTM, D_MLSTM, H_M, H_M, D_MLSTM, D_MLSTM, Q_LORA, KV_LORA, ROPE, D_ATT, D_MODEL, D_MODEL)

LANES = 128
SUBLANES = 8
VMEM_LIMIT_BYTES = 56 * 1024 * 1024


def _const_spec(shape):
    nd = len(shape)
    return pl.BlockSpec(shape, lambda *_: (0,) * nd, pipeline_mode=pl.Buffered(1))


def _dot(a, b):
    return jnp.dot(a, b, preferred_element_type=F32)


def _dot_nt(a, b):
    return lax.dot_general(a, b, (((1,), (1,)), ((), ())), preferred_element_type=F32)


def _rms_scale(x, width):
    return lax.rsqrt(jnp.sum(x * x, axis=-1, keepdims=True) * (1.0 / width) + EPS)


def _sigmoid(x):
    return 1.0 / (1.0 + jnp.exp(-x))


def _split3(x):
    a = x.astype(BF16)
    r = x - a.astype(F32)
    b = r.astype(BF16)
    c = (r - b.astype(F32)).astype(BF16)
    return a, b, c


def _rope_half_norm(t, gain, gain_rolled, cos_t, sin_t):
    lane = lax.broadcasted_iota(jnp.int32, t.shape, 1)
    sq = jnp.where(lane < ROPE, t * t, 0.0)
    r = lax.rsqrt(jnp.sum(sq, axis=-1, keepdims=True) * (1.0 / ROPE) + EPS)
    t_sw = pltpu.roll(t, ROPE, axis=1)
    return r * (t * gain * cos_t + t_sw * gain_rolled * sin_t)


def _inproj_kernel(*refs, emit_kv):
    (x_ref, ng_ref, wbig_ref, wcq_ref, wsm_ref, qng_ref, wuq_ref, kvng_ref, gkr_ref, gkr2_ref,
     gqn_ref, gqr_ref, gqr2_ref, wukv_ref, gkn_ref, cos_ref, sin_ref, negb_ref,
     xcm_ref, vm_ref, og_ref, sza_ref, sgm_ref, sga_ref, gates_ref, ckv_ref, kr_ref, q_ref,
     tail_ref, *kv_refs) = refs
    x = x_ref[...]
    h = ((x * _rms_scale(x, D_MODEL)) * ng_ref[...]).astype(BF16)
    cos_t = cos_ref[...]
    sin_t = sin_ref[...]
    lane = lax.broadcasted_iota(jnp.int32, (x.shape[0], LANES), 1)

    def seg(i):
        return _dot(h, wbig_ref[:, i * D_MODEL:(i + 1) * D_MODEL])

    cq = _dot(h, wcq_ref[...])
    cqn = ((cq * _rms_scale(cq, Q_LORA)) * qng_ref[...]).astype(BF16)
    qscale = ATT_SCALE * LOG2E
    q_aug = jnp.where(lane == ROPE, negb_ref[0], 0.0)

    def q_head(hd):
        qh = _dot(cqn, wuq_ref[:, hd * 2 * LANES:(hd + 1) * 2 * LANES])
        qn = qh[:, :NOPE]
        qn = (qn * _rms_scale(qn, NOPE)) * gqn_ref[...]
        q_ref[0, hd, :, :NOPE] = (qn * qscale).astype(BF16)
        qr = _rope_half_norm(qh[:, NOPE:], gqr_ref[...], gqr2_ref[...], cos_t, sin_t)
        q_ref[0, hd, :, NOPE:] = jnp.where(lane < ROPE, qr * qscale, q_aug).astype(BF16)

    t = _dot(h, wsm_ref[...])
    c = t[:, :KV_LORA]
    ckv = (c * _rms_scale(c, KV_LORA)) * kvng_ref[...]
    ckv_ref[...] = ckv
    kr = _rope_half_norm(t[:, KV_LORA:KV_LORA + LANES], gkr_ref[...], gkr2_ref[...], cos_t, sin_t)
    kr_ref[...] = kr[:, :ROPE]
    gates_ref[...] = t[:, KV_LORA + LANES:]

    if emit_kv:
        k_ref, v_ref = kv_refs
        ckv_b = ckv.astype(BF16)
        kr_aug = jnp.where(lane < ROPE, kr, (lane == ROPE).astype(F32)).astype(BF16)

        def kv_head(hd):
            _kv_head(hd, ckv_b, kr_aug, wukv_ref, gkn_ref, k_ref, v_ref)
    else:
        def kv_head(hd):
            pass

    q_head(0)
    kv_head(0)
    z = seg(3)
    q_head(1)
    kv_head(1)
    og_ref[...] = (_sigmoid(seg(2)) * (z * _sigmoid(z))).astype(BF16)
    q_head(2)
    kv_head(2)
    z = seg(4)
    sza_ref[...] = (z * _sigmoid(z)).astype(BF16)
    q_head(3)
    kv_head(3)
    sgm_ref[...] = _sigmoid(seg(5)).astype(BF16)
    q_head(4)
    kv_head(4)
    sga_ref[...] = _sigmoid(seg(6)).astype(BF16)
    q_head(5)
    kv_head(5)
    q_head(6)
    kv_head(6)
    q_head(7)
    kv_head(7)
    xcm = seg(0)
    xcm_ref[...] = xcm.astype(BF16)
    tail_ref[0] = xcm[xcm.shape[0] - SUBLANES:, :]
    vm_ref[...] = seg(1).astype(BF16)


def _kv_head(hd, ckv_b, kr_aug, w_ref, gkn_ref, k_ref, v_ref):
    kv = _dot(ckv_b, w_ref[:, hd * 2 * LANES:(hd + 1) * 2 * LANES])
    kn = kv[:, :NOPE]
    k_ref[0, hd, :, :NOPE] = ((kn * _rms_scale(kn, NOPE)) * gkn_ref[...]).astype(BF16)
    k_ref[0, hd, :, NOPE:] = kr_aug
    lane = lax.broadcasted_iota(jnp.int32, kv.shape, 1)
    v_aug = jnp.where(lane < V_DIM, pltpu.roll(kv, V_DIM, axis=1), (lane == V_DIM).astype(F32))
    v_ref[0, hd] = v_aug.astype(BF16)


def _inproj(x, prm, cos_t, sin_t, negb, tm, emit_kv):
    B, L, _ = x.shape
    T = B * L
    tps = L // tm
    x2 = x.reshape(T, D_MODEL)
    row = lambda w: pl.BlockSpec((tm, w), lambda i: (i, 0))
    pos = pl.BlockSpec((tm, LANES), lambda i: (i % tps, 0))
    head_rows = lambda w: pl.BlockSpec((1, H_A, tm, w), lambda i: (i // tps, 0, i % tps, 0))
    consts = [prm["norm_g"], prm["w_big"], prm["w_cq"], prm["w_small"], prm["qn_g"], prm["w_uq"],
              prm["kvn_g"], prm["g_kr"], prm["g_kr2"], prm["g_qn"], prm["g_qr"], prm["g_qr2"],
              prm["w_ukv"], prm["g_kn"]]
    wide = jax.ShapeDtypeStruct((T, D_MODEL), BF16)
    heads = lambda w: jax.ShapeDtypeStruct((B, H_A, L, w), BF16)
    out_shape = [wide] * 6 + [
        jax.ShapeDtypeStruct((T, LANES), F32),
        jax.ShapeDtypeStruct((T, KV_LORA), F32),
        jax.ShapeDtypeStruct((T, ROPE), F32),
        heads(QK_PAD),
        jax.ShapeDtypeStruct((B, SUBLANES, D_MLSTM), F32),
    ] + ([heads(QK_PAD), heads(V_PAD)] if emit_kv else [])
    out_specs = [row(D_MODEL)] * 6 + [
        row(LANES), row(KV_LORA), row(ROPE), head_rows(QK_PAD),
        pl.BlockSpec((1, SUBLANES, D_MLSTM), lambda i: (i // tps, 0, 0)),
    ] + ([head_rows(QK_PAD), head_rows(V_PAD)] if emit_kv else [])
    return pl.pallas_call(
        functools.partial(_inproj_kernel, emit_kv=emit_kv),
        grid=(T // tm,),
        in_specs=[row(D_MODEL)] + [_const_spec(c.shape) for c in consts] + [
            pos, pos, pl.BlockSpec((1, 1, LANES), lambda i: (i // tps, 0, 0))],
        out_specs=out_specs,
        out_shape=out_shape,
        compiler_params=pltpu.CompilerParams(
            dimension_semantics=("arbitrary",), vmem_limit_bytes=VMEM_LIMIT_BYTES),
        name="inproj",
    )(x2, *consts, cos_t, sin_t, negb)


def _kvup_kernel(ckv_ref, kr_ref, w_ref, gkn_ref, k_ref, v_ref):
    ckv_b = ckv_ref[0].astype(BF16)
    kr_aug = kr_ref[0].astype(BF16)
    for hd in range(H_A):
        _kv_head(hd, ckv_b, kr_aug, w_ref, gkn_ref, k_ref, v_ref)


def _kvup(ckv, kr, prm, tm):
    B, T, _ = ckv.shape
    return pl.pallas_call(
        _kvup_kernel,
        grid=(B, T // tm),
        in_specs=[pl.BlockSpec((1, tm, KV_LORA), lambda b, i: (b, i, 0)),
                  pl.BlockSpec((1, tm, LANES), lambda b, i: (b, i, 0)),
                  _const_spec(prm["w_ukv"].shape), _const_spec(prm["g_kn"].shape)],
        out_specs=[pl.BlockSpec((1, H_A, tm, QK_PAD), lambda b, i: (b, 0, i, 0)),
                   pl.BlockSpec((1, H_A, tm, V_PAD), lambda b, i: (b, 0, i, 0))],
        out_shape=[jax.ShapeDtypeStruct((B, H_A, T, QK_PAD), BF16),
                   jax.ShapeDtypeStruct((B, H_A, T, V_PAD), BF16)],
        compiler_params=pltpu.CompilerParams(
            dimension_semantics=("arbitrary", "arbitrary"), vmem_limit_bytes=VMEM_LIMIT_BYTES),
        name="kvup",
    )(ckv, kr, prm["w_ukv"], prm["g_kn"])


HEADS_PER_STEP = 4


def _attn_kernel(safe_ref, q_ref, k_ref, v_ref, o_ref, acc_ref, m_ref, *, tq, wide, past):
    b, qi = pl.program_id(0), pl.program_id(2)
    diag0 = past + qi * tq
    heads = range(HEADS_PER_STEP)

    def chunk_mask(width):
        rows = lax.broadcasted_iota(jnp.int32, (tq, width), 0) // CHUNK
        cols = lax.broadcasted_iota(jnp.int32, (tq, width), 1) // CHUNK
        return cols <= rows + (width - tq) // CHUNK

    acc_ref[...] = jnp.zeros_like(acc_ref)

    @pl.when(safe_ref[b] == 1)
    def _bounded():
        def step(ks, size, msk):
            scores = [_dot_nt(q_ref[0, h], k_ref[0, h, pl.ds(ks, size), :]) for h in heads]
            for h in heads:
                s = scores[h] if msk is None else jnp.where(msk, scores[h], NEG)
                p = jnp.exp2(s).astype(BF16)
                acc_ref[h] += _dot(p, v_ref[0, h, pl.ds(ks, size), :])

        @pl.loop(0, diag0 // wide)
        def _(j):
            step(pl.multiple_of(j * wide, wide), wide, None)

        @pl.when(diag0 % wide == 0)
        def _():
            step(pl.multiple_of(diag0, tq), tq, chunk_mask(tq))

        @pl.when(diag0 % wide != 0)
        def _():
            step(pl.multiple_of(diag0 - tq, tq), 2 * tq, chunk_mask(2 * tq))

    @pl.when(safe_ref[b] == 0)
    def _online():
        m_ref[...] = jnp.full_like(m_ref, NEG)

        def step(ks, msk):
            for h in heads:
                s = _dot_nt(q_ref[0, h], k_ref[0, h, pl.ds(ks, tq), :])
                if msk is not None:
                    s = jnp.where(msk, s, NEG)
                m_old = m_ref[h]
                m_new = jnp.maximum(m_old, jnp.max(s, axis=1, keepdims=True))
                p = jnp.exp2(s - m_new).astype(BF16)
                acc_ref[h] = (jnp.exp2(m_old - m_new) * acc_ref[h]
                              + _dot(p, v_ref[0, h, pl.ds(ks, tq), :]))
                m_ref[h] = m_new

        @pl.loop(0, diag0 // tq)
        def _(j):
            step(pl.multiple_of(j * tq, tq), None)

        step(pl.multiple_of(diag0, tq), chunk_mask(tq))

    for h in heads:
        acc = acc_ref[h]
        o_ref[0, :, h * V_DIM:(h + 1) * V_DIM] = (
            acc[:, :V_DIM] / acc[:, V_DIM:V_DIM + 1]).astype(BF16)


def _attention(q, k, v, safe, tq, wide):
    B, H, L, _ = q.shape
    T = k.shape[2]
    past = T - L
    hp = HEADS_PER_STEP
    assert past % wide == 0 and (wide == 2 * tq or L == tq) and L % tq == 0 and H % hp == 0
    grid_spec = pltpu.PrefetchScalarGridSpec(
        num_scalar_prefetch=1,
        grid=(B, H // hp, L // tq),
        in_specs=[pl.BlockSpec((1, hp, tq, QK_PAD), lambda b, h, i, s: (b, h, i, 0)),
                  pl.BlockSpec((1, hp, T, QK_PAD), lambda b, h, i, s: (b, h, 0, 0)),
                  pl.BlockSpec((1, hp, T, V_PAD), lambda b, h, i, s: (b, h, 0, 0))],
        out_specs=pl.BlockSpec((1, tq, hp * V_DIM), lambda b, h, i, s: (b, i, h)),
        scratch_shapes=[pltpu.VMEM((hp, tq, V_PAD), F32), pltpu.VMEM((hp, tq, 1), F32)],
    )
    return pl.pallas_call(
        functools.partial(_attn_kernel, tq=tq, wide=wide, past=past),
        grid_spec=grid_spec,
        out_shape=jax.ShapeDtypeStruct((B, L, D_ATT), BF16),
        compiler_params=pltpu.CompilerParams(
            dimension_semantics=("arbitrary", "arbitrary", "arbitrary"),
            vmem_limit_bytes=VMEM_LIMIT_BYTES),
        name="attention",
    )(safe, q, k, v)


def _mlstm_kernel(xcm_ref, vm_ref, gates_ref, og_ref, sgm_ref, o_ref, sza_ref, sga_ref, x_ref,
                  cbuf_ref, c0_ref, n0_ref, m0_ref, bif_ref, cw_ref, cb_ref, wq_ref, wkt_ref,
                  hg_ref, wpm_ref, wpa_ref, wout_ref, tri_ref,
                  y_ref, c_ref, n_ref, m_ref, xpad_ref, hn_ref, *, ch):
    ci = pl.program_id(1)

    @pl.when(ci == 0)
    def _():
        c_ref[...] = c0_ref[...]
        n_ref[...] = n0_ref[...]
        m_ref[...] = m0_ref[...]
        xpad_ref[0:SUBLANES, :] = cbuf_ref[0]

    pa = _dot(o_ref[0] * sza_ref[0], wpa_ref[...])

    xpad_ref[SUBLANES:, :] = xcm_ref[0].astype(F32)

    gpre = gates_ref[0] + bif_ref[...]
    lane = lax.broadcasted_iota(jnp.int32, (ch, LANES), 1)
    lf = jnp.minimum(gpre, 0.0) - jnp.log1p(jnp.exp(-jnp.abs(gpre)))
    lf = jnp.where((lane >= H_M) & (lane < 2 * H_M), lf, 0.0)
    tri = (lax.broadcasted_iota(jnp.int32, (ch, ch), 1)
           <= lax.broadcasted_iota(jnp.int32, (ch, ch), 0))
    bcum = sum(_dot(tri_ref[...], part) for part in _split3(lf))
    xmat = jnp.where(lane < H_M, gpre, -bcum)
    if ch % LANES:
        xmat = jnp.concatenate([xmat, jnp.zeros((LANES - ch % LANES, LANES), F32)], axis=0)
    xrow = xmat.T

    def project(hd):
        sl = slice(hd * DH_M, (hd + 1) * DH_M)
        u = cb_ref[:, sl] + xpad_ref[SUBLANES:, sl] * cw_ref[CONV_W - 1:CONV_W, sl]
        for j in range(CONV_W - 1):
            off = SUBLANES - (CONV_W - 1) + j
            u = u + xpad_ref[off:off + ch, sl] * cw_ref[j:j + 1, sl]
        su_h = (u * _sigmoid(u)).astype(BF16)
        qh = _dot(su_h, wq_ref[hd])
        kt = _dot_nt(wkt_ref[hd], su_h)
        qb = qh.astype(BF16)
        c0 = c_ref[0, hd]
        return qh, kt, _dot(qb, kt.astype(BF16)), _dot(qb, c0.astype(BF16)), c0

    def recur(hd, qh, kt, s_raw, q_c0, c0):
        sl = slice(hd * DH_M, (hd + 1) * DH_M)
        m0 = m_ref[0, hd:hd + 1, 0:1]
        bc = bcum[:, H_M + hd:H_M + hd + 1]
        drow = xrow[hd:hd + 1, :ch] + xrow[H_M + hd:H_M + hd + 1, :ch]
        dlog = jnp.where(tri, bc + drow, NEG)
        g = bc + m0
        m = jnp.maximum(g, jnp.max(dlog, axis=1, keepdims=True))
        w_intra = jnp.exp(dlog - m)
        w_inter = jnp.exp(g - m)

        v = vm_ref[0, :, sl]
        n0 = n_ref[0, hd:hd + 1, :]
        s = s_raw * w_intra
        num = w_inter * q_c0 + _dot(s.astype(BF16), v)
        den = (w_inter * jnp.sum(qh * n0, axis=1, keepdims=True)
               + jnp.sum(s, axis=1, keepdims=True))
        hh = num / jnp.maximum(jnp.abs(den), jnp.exp(-m))
        hn_ref[:, sl] = ((hh * _rms_scale(hh, DH_M)) * hg_ref[:, sl]).astype(BF16)

        m_last = m[ch - 1:ch, :]
        b_last = bc[ch - 1:ch, :]
        w_state = jnp.exp(drow + (b_last - m_last))
        decay = jnp.exp(b_last + m0 - m_last)
        c_ref[0, hd] = decay * c0 + _dot((kt * w_state).astype(BF16), v)
        w8 = jnp.broadcast_to(w_state, (SUBLANES, ch)).astype(BF16)
        n_ref[0, hd:hd + 1, :] = decay * n0 + _dot_nt(w8, kt.astype(BF16))[0:1, :]
        m_ref[0, hd:hd + 1, :] = jnp.broadcast_to(m_last, (1, LANES))

    pending = project(0)
    for hd in range(H_M):
        ahead = project(hd + 1) if hd + 1 < H_M else None
        recur(hd, *pending)
        pending = ahead
    xpad_ref[0:SUBLANES, :] = xpad_ref[ch:ch + SUBLANES, :]

    hm = hn_ref[...] * og_ref[0]
    pm = _dot(hm, wpm_ref[...])
    merged = sgm_ref[0] * pm.astype(BF16) + sga_ref[0] * pa.astype(BF16)
    y_ref[0] = x_ref[0] + _dot(merged, wout_ref[...])


def _mlstm(xcm, vm, gates, og, sgm, o, sza, sga, x, cbuf, c0, n0, m0b, prm, ch):
    B, L, _ = xcm.shape
    seq = lambda w: pl.BlockSpec((1, ch, w), lambda b, c: (b, c, 0))
    per_b = lambda *s: pl.BlockSpec((1,) + s, lambda b, c: (b,) + (0,) * len(s))
    consts = [prm["b_if"], prm["conv_w"], prm["conv_b"], prm["wq_m"], prm["wk_m"], prm["hnorm_g"],
              prm["w_pm"], prm["w_pa"], prm["w_out"],
              jnp.tril(jnp.ones((ch, ch), BF16))]
    return pl.pallas_call(
        functools.partial(_mlstm_kernel, ch=ch),
        grid=(B, L // ch),
        in_specs=[seq(D_MLSTM), seq(D_MLSTM), seq(LANES), seq(D_MLSTM), seq(D_MODEL),
                  seq(D_ATT), seq(D_ATT), seq(D_MODEL), seq(D_MODEL),
                  per_b(SUBLANES, D_MLSTM), per_b(H_M, DH_M, DH_M), per_b(H_M, DH_M),
                  per_b(H_M, LANES)] + [_const_spec(c.shape) for c in consts],
        out_specs=[seq(D_MODEL), per_b(H_M, DH_M, DH_M), per_b(H_M, DH_M), per_b(H_M, LANES)],
        out_shape=[jax.ShapeDtypeStruct((B, L, D_MODEL), F32),
                   jax.ShapeDtypeStruct((B, H_M, DH_M, DH_M), F32),
                   jax.ShapeDtypeStruct((B, H_M, DH_M), F32),
                   jax.ShapeDtypeStruct((B, H_M, LANES), F32)],
        scratch_shapes=[pltpu.VMEM((ch + SUBLANES, D_MLSTM), F32),
                        pltpu.VMEM((ch, D_MLSTM), BF16)],
        compiler_params=pltpu.CompilerParams(
            dimension_semantics=("arbitrary", "arbitrary"), vmem_limit_bytes=VMEM_LIMIT_BYTES),
        name="mlstm",
    )(xcm, vm, gates, og, sgm, o, sza, sga, x, cbuf, c0, n0, m0b, *consts)


def _swap_halves(a):
    half = ROPE // 2
    return jnp.concatenate([a[..., half:], a[..., :half]], axis=-1)


def _prepare_params(norm_g, w_in, b_if, conv_w, conv_b, wq_m, wk_m, hnorm_g, qn_g, w_uq, kvn_g,
                    w_ukv, g_qn, g_qr, g_kn, g_kr, w_pm, w_pa, w_out):
    cols, start = [], 0
    for w in SPLITS:
        cols.append(w_in[:, start:start + w])
        start += w
    xcm, vm, ig, fg, og, zm, cq, ckv, kr, za, gm, ga = cols
    pad = jnp.zeros((D_MODEL, LANES - 2 * H_M), F32)
    w_small = jnp.concatenate([ckv, kr, _swap_halves(kr), ig, fg, pad], axis=1)
    wuq = w_uq.reshape(Q_LORA, H_A, QK_DIM)
    wuq = jnp.concatenate([wuq, _swap_halves(wuq[..., NOPE:])], axis=-1)
    row = lambda a: a.reshape(1, -1).astype(F32)
    pair = lambda g: row(jnp.concatenate([g, _swap_halves(g)]))
    pair2 = lambda g: row(jnp.concatenate([_swap_halves(g), g]))
    return {
        "norm_g": row(norm_g),
        "w_big": jnp.concatenate([xcm, vm, og, zm, za, gm, ga], axis=1).astype(BF16),
        "w_cq": cq.astype(BF16),
        "w_small": w_small.astype(BF16),
        "qn_g": row(qn_g),
        "w_uq": wuq.reshape(Q_LORA, H_A * 2 * LANES).astype(BF16),
        "kvn_g": row(kvn_g),
        "g_kr": pair(g_kr), "g_kr2": pair2(g_kr),
        "g_qn": row(g_qn),
        "g_qr": pair(g_qr), "g_qr2": pair2(g_qr),
        "w_ukv": w_ukv.astype(BF16),
        "g_kn": row(g_kn),
        "q_bound2": (ATT_SCALE * LOG2E) ** 2 * (NOPE * jnp.max(g_qn * g_qn)
                                                + ROPE * jnp.max(g_qr * g_qr)),
        "kn_bound2": NOPE * jnp.max(g_kn * g_kn),
        "kr_bound2": ROPE * jnp.max(g_kr * g_kr),
        "b_if": row(jnp.concatenate([b_if, jnp.zeros((LANES - 2 * H_M,), F32)])),
        "conv_w": conv_w, "conv_b": row(conv_b),
        "wq_m": wq_m.astype(BF16),
        "wk_m": (jnp.swapaxes(wk_m, 1, 2) * (DH_M ** -0.5)).astype(BF16),
        "hnorm_g": row(hnorm_g),
        "w_pm": w_pm.astype(BF16), "w_pa": w_pa.astype(BF16), "w_out": w_out.astype(BF16),
    }


def _rope_tables(past, length):
    half = ROPE // 2
    inv = jnp.power(ROPE_THETA, -jnp.arange(half, dtype=F32) / half)
    ang = (past + jnp.arange(length)).astype(F32)[:, None] * inv[None, :]
    cos, sin = jnp.cos(ang), jnp.sin(ang)
    return jnp.tile(cos, (1, 4)), jnp.tile(jnp.concatenate([-sin, sin], axis=1), (1, 2))


def _hybrid_layer(x, past_ckv, past_kr, conv_buf, c0, n0, m0, prm, *, tm, tq, wide, ch):
    B, L, _ = x.shape
    past = past_ckv.shape[1]
    cos_t, sin_t = _rope_tables(past, L)
    seq = lambda a: a.reshape(B, L, a.shape[-1])

    kr2 = jnp.broadcast_to(prm["kr_bound2"], (B,))
    if past:
        pk = past_kr.astype(F32)
        kr2 = jnp.maximum(kr2, jnp.max(jnp.sum(pk * pk, axis=-1), axis=1))
    bound = jnp.sqrt((prm["kn_bound2"] + kr2 + 1.0) * prm["q_bound2"]) * 1.01
    safe = bound <= SAFE_BOUND
    negb = jnp.broadcast_to(jnp.where(safe, -bound, 0.0)[:, None, None], (B, 1, LANES))

    outs = _inproj(x, prm, cos_t, sin_t, negb, tm, emit_kv=not past)
    xcm, vm, og, sza, sgm, sga, gates, ckv_new, kr_new, q, tail = outs[:11]
    ckv_new, kr_new = seq(ckv_new), seq(kr_new)
    if past:
        T = past + L
        aug = jnp.concatenate(
            [jnp.ones((B, T, 1), F32), jnp.zeros((B, T, LANES - ROPE - 1), F32)], axis=-1)
        kr_all = jnp.concatenate([jnp.concatenate([past_kr, kr_new], axis=1), aug], axis=-1)
        k, v = _kvup(jnp.concatenate([past_ckv, ckv_new], axis=1), kr_all, prm, T)
    else:
        k, v = outs[11:]
    o = _attention(q, k, v, safe.astype(jnp.int32), tq, wide)

    cbuf = jnp.concatenate(
        [jnp.zeros((B, SUBLANES - (CONV_W - 1), D_MLSTM), F32), conv_buf.astype(F32)], axis=1)
    m0b = jnp.broadcast_to(m0.astype(F32)[:, :, None], (B, H_M, LANES))
    y, c1, n1, m1 = _mlstm(seq(xcm), seq(vm), seq(gates), seq(og), seq(sgm), o, seq(sza), seq(sga),
                           x, cbuf, c0.astype(F32), n0.astype(F32), m0b, prm, ch)
    conv_new = tail[:, SUBLANES - (CONV_W - 1):, :]
    return y, ckv_new, kr_new, conv_new, c1, n1, m1[:, :, 0]


def kernel(x_prompt, x_sample, cache_ckv, cache_kr, state_conv, state_C, state_n, state_m,
           norm_g, w_in, b_if, conv_w, conv_b, wq_m, wk_m, hnorm_g,
           qn_g, w_uq, kvn_g, w_ukv, g_qn, g_qr, g_kn, g_kr, w_pm, w_pa, w_out):
    dt = x_prompt.dtype
    B = x_prompt.shape[0]
    depth = norm_g.shape[0]
    yp, ys = x_prompt, x_sample
    outs_p = [[] for _ in range(6)]
    outs_s = [[] for _ in range(6)]
    for l in range(depth):
        prm = _prepare_params(norm_g[l], w_in[l], b_if[l], conv_w[l], conv_b[l], wq_m[l], wk_m[l],
                              hnorm_g[l], qn_g[l], w_uq[l], kvn_g[l], w_ukv[l], g_qn[l], g_qr[l],
                              g_kn[l], g_kr[l], w_pm[l], w_pa[l], w_out[l])
        yp, *res_p = _hybrid_layer(
            yp, jnp.zeros((B, 0, KV_LORA), dt), jnp.zeros((B, 0, ROPE), dt),
            jnp.zeros((B, CONV_W - 1, D_MLSTM), dt), jnp.zeros((B, H_M, DH_M, DH_M), F32),
            jnp.zeros((B, H_M, DH_M), F32), jnp.zeros((B, H_M), F32), prm,
            tm=512, tq=512, wide=1024, ch=256)
        Ls = x_sample.shape[1]
        ys, *res_s = _hybrid_layer(
            ys, cache_ckv[l], cache_kr[l], state_conv[l], state_C[l], state_n[l], state_m[l], prm,
            tm=Ls, tq=Ls, wide=cache_ckv.shape[2], ch=Ls)
        for lst, a in zip(outs_p, res_p):
            lst.append(a.astype(dt))
        for lst, a in zip(outs_s, res_s):
            lst.append(a.astype(dt))
    return (yp, ys) + tuple(jnp.stack(a) for a in outs_p) + tuple(jnp.stack(a) for a in outs_s)
```

```python
import functools
import math

import jax
import jax.numpy as jnp
from jax import lax
from jax.experimental import pallas as pl
from jax.experimental.pallas import tpu as pltpu

F32 = jnp.float32
BF16 = jnp.bfloat16

D_MODEL = 1024
CHUNK = 64
EPS = 1e-6
NEG = -1e30
D_MLSTM = D_MODEL
H_M = 4
DH_M = D_MLSTM // H_M
CONV_W = 4
H_A = 8
NOPE = 128
ROPE = 64
V_DIM = 128
Q_LORA = 384
KV_LORA = 256
D_ATT = H_A * V_DIM
ROPE_THETA = 10000.0
ATT_SCALE = (NOPE + ROPE) ** -0.5
LOG2E = math.log2(math.e)
QK_DIM = NOPE + ROPE
QK_PAD = 256
V_PAD = 256
SAFE_BOUND = 50.0
SPLITS = (D_MLSTM, D_MLSTM, H_M, H_M, D_MLSTM, D_MLSTM, Q_LORA, KV_LORA, ROPE, D_ATT, D_MODEL, D_MODEL)

LANES = 128
SUBLANES = 8
VMEM_LIMIT_BYTES = 56 * 1024 * 1024


def _const_spec(shape):
    nd = len(shape)
    return pl.BlockSpec(shape, lambda *_: (0,) * nd, pipeline_mode=pl.Buffered(1))


def _dot(a, b):
    return jnp.dot(a, b, preferred_element_type=F32)


def _dot_nt(a, b):
    return lax.dot_general(a, b, (((1,), (1,)), ((), ())), preferred_element_type=F32)


def _rms_scale(x, width):
    return lax.rsqrt(jnp.sum(x * x, axis=-1, keepdims=True) * (1.0 / width) + EPS)


def _sigmoid(x):
    return 0.5 * jnp.tanh(0.5 * x) + 0.5


def _split3(x):
    a = x.astype(BF16)
    r = x - a.astype(F32)
    b = r.astype(BF16)
    c = (r - b.astype(F32)).astype(BF16)
    return a, b, c


def _rope_half_norm(t, gain, gain_rolled, cos_t, sin_t):
    lane = lax.broadcasted_iota(jnp.int32, t.shape, 1)
    sq = jnp.where(lane < ROPE, t * t, 0.0)
    r = lax.rsqrt(jnp.sum(sq, axis=-1, keepdims=True) * (1.0 / ROPE) + EPS)
    t_sw = pltpu.roll(t, ROPE, axis=1)
    return r * (t * gain * cos_t + t_sw * gain_rolled * sin_t)


def _inproj_kernel(*refs, emit_kv):
    (x_ref, ng_ref, wbig_ref, wcq_ref, wsm_ref, qng_ref, wuq_ref, kvng_ref, gkr_ref, gkr2_ref,
     gqn_ref, gqr_ref, gqr2_ref, wukv_ref, gkn_ref, cos_ref, sin_ref, negb_ref,
     xcm_ref, vm_ref, og_ref, sza_ref, sgm_ref, sga_ref, gates_ref, ckv_ref, kr_ref, q_ref,
     tail_ref, *kv_refs) = refs
    x = x_ref[...]
    h = ((x * _rms_scale(x, D_MODEL)) * ng_ref[...]).astype(BF16)
    cos_t = cos_ref[...]
    sin_t = sin_ref[...]
    lane = lax.broadcasted_iota(jnp.int32, (x.shape[0], LANES), 1)

    def seg(i):
        return _dot(h, wbig_ref[:, i * D_MODEL:(i + 1) * D_MODEL])

    cq = _dot(h, wcq_ref[...])
    cqn = ((cq * _rms_scale(cq, Q_LORA)) * qng_ref[...]).astype(BF16)
    qscale = ATT_SCALE * LOG2E
    q_aug = jnp.where(lane == ROPE, negb_ref[0], 0.0)

    def q_head(hd):
        qh = _dot(cqn, wuq_ref[:, hd * 2 * LANES:(hd + 1) * 2 * LANES])
        qn = qh[:, :NOPE]
        qn = (qn * _rms_scale(qn, NOPE)) * gqn_ref[...]
        q_ref[0, hd, :, :NOPE] = (qn * qscale).astype(BF16)
        qr = _rope_half_norm(qh[:, NOPE:], gqr_ref[...], gqr2_ref[...], cos_t, sin_t)
        q_ref[0, hd, :, NOPE:] = jnp.where(lane < ROPE, qr * qscale, q_aug).astype(BF16)

    t = _dot(h, wsm_ref[...])
    c = t[:, :KV_LORA]
    ckv = (c * _rms_scale(c, KV_LORA)) * kvng_ref[...]
    ckv_ref[...] = ckv
    kr = _rope_half_norm(t[:, KV_LORA:KV_LORA + LANES], gkr_ref[...], gkr2_ref[...], cos_t, sin_t)
    kr_ref[...] = kr[:, :ROPE]
    gates_ref[...] = t[:, KV_LORA + LANES:]

    if emit_kv:
        k_ref, v_ref = kv_refs
        ckv_b = ckv.astype(BF16)
        kr_aug = jnp.where(lane < ROPE, kr, (lane == ROPE).astype(F32)).astype(BF16)

        def kv_head(hd):
            _kv_head(hd, ckv_b, kr_aug, wukv_ref, gkn_ref, k_ref, v_ref)
    else:
        def kv_head(hd):
            pass

    q_head(0)
    kv_head(0)
    z = seg(3)
    q_head(1)
    kv_head(1)
    og_ref[...] = (_sigmoid(seg(2)) * (z * _sigmoid(z))).astype(BF16)
    q_head(2)
    kv_head(2)
    z = seg(4)
    sza_ref[...] = (z * _sigmoid(z)).astype(BF16)
    q_head(3)
    kv_head(3)
    sgm_ref[...] = _sigmoid(seg(5)).astype(BF16)
    q_head(4)
    kv_head(4)
    sga_ref[...] = _sigmoid(seg(6)).astype(BF16)
    q_head(5)
    kv_head(5)
    q_head(6)
    kv_head(6)
    q_head(7)
    kv_head(7)
    xcm = seg(0)
    xcm_ref[...] = xcm.astype(BF16)
    tail_ref[0] = xcm[xcm.shape[0] - SUBLANES:, :]
    vm_ref[...] = seg(1).astype(BF16)


def _kv_head(hd, ckv_b, kr_aug, w_ref, gkn_ref, k_ref, v_ref):
    kv = _dot(ckv_b, w_ref[:, hd * 2 * LANES:(hd + 1) * 2 * LANES])
    kn = kv[:, :NOPE]
    k_ref[0, hd, :, :NOPE] = ((kn * _rms_scale(kn, NOPE)) * gkn_ref[...]).astype(BF16)
    k_ref[0, hd, :, NOPE:] = kr_aug
    lane = lax.broadcasted_iota(jnp.int32, kv.shape, 1)
    v_aug = jnp.where(lane < V_DIM, pltpu.roll(kv, V_DIM, axis=1), (lane == V_DIM).astype(F32))
    v_ref[0, hd] = v_aug.astype(BF16)


def _inproj(x, prm, cos_t, sin_t, negb, tm, emit_kv):
    B, L, _ = x.shape
    T = B * L
    tps = L // tm
    x2 = x.reshape(T, D_MODEL)
    row = lambda w: pl.BlockSpec((tm, w), lambda i: (i, 0))
    pos = pl.BlockSpec((tm, LANES), lambda i: (i % tps, 0))
    head_rows = lambda w: pl.BlockSpec((1, H_A, tm, w), lambda i: (i // tps, 0, i % tps, 0))
    consts = [prm["norm_g"], prm["w_big"], prm["w_cq"], prm["w_small"], prm["qn_g"], prm["w_uq"],
              prm["kvn_g"], prm["g_kr"], prm["g_kr2"], prm["g_qn"], prm["g_qr"], prm["g_qr2"],
              prm["w_ukv"], prm["g_kn"]]
    wide = jax.ShapeDtypeStruct((T, D_MODEL), BF16)
    heads = lambda w: jax.ShapeDtypeStruct((B, H_A, L, w), BF16)
    out_shape = [wide] * 6 + [
        jax.ShapeDtypeStruct((T, LANES), F32),
        jax.ShapeDtypeStruct((T, KV_LORA), F32),
        jax.ShapeDtypeStruct((T, ROPE), F32),
        heads(QK_PAD),
        jax.ShapeDtypeStruct((B, SUBLANES, D_MLSTM), F32),
    ] + ([heads(QK_PAD), heads(V_PAD)] if emit_kv else [])
    out_specs = [row(D_MODEL)] * 6 + [
        row(LANES), row(KV_LORA), row(ROPE), head_rows(QK_PAD),
        pl.BlockSpec((1, SUBLANES, D_MLSTM), lambda i: (i // tps, 0, 0)),
    ] + ([head_rows(QK_PAD), head_rows(V_PAD)] if emit_kv else [])
    return pl.pallas_call(
        functools.partial(_inproj_kernel, emit_kv=emit_kv),
        grid=(T // tm,),
        in_specs=[row(D_MODEL)] + [_const_spec(c.shape) for c in consts] + [
            pos, pos, pl.BlockSpec((1, 1, LANES), lambda i: (i // tps, 0, 0))],
        out_specs=out_specs,
        out_shape=out_shape,
        compiler_params=pltpu.CompilerParams(
            dimension_semantics=("arbitrary",), vmem_limit_bytes=VMEM_LIMIT_BYTES),
        name="inproj",
    )(x2, *consts, cos_t, sin_t, negb)


def _kvup_kernel(ckv_ref, kr_ref, w_ref, gkn_ref, k_ref, v_ref):
    ckv_b = ckv_ref[0].astype(BF16)
    kr_aug = kr_ref[0].astype(BF16)
    for hd in range(H_A):
        _kv_head(hd, ckv_b, kr_aug, w_ref, gkn_ref, k_ref, v_ref)


def _kvup(ckv, kr, prm, tm):
    B, T, _ = ckv.shape
    return pl.pallas_call(
        _kvup_kernel,
        grid=(B, T // tm),
        in_specs=[pl.BlockSpec((1, tm, KV_LORA), lambda b, i: (b, i, 0)),
                  pl.BlockSpec((1, tm, LANES), lambda b, i: (b, i, 0)),
                  _const_spec(prm["w_ukv"].shape), _const_spec(prm["g_kn"].shape)],
        out_specs=[pl.BlockSpec((1, H_A, tm, QK_PAD), lambda b, i: (b, 0, i, 0)),
                   pl.BlockSpec((1, H_A, tm, V_PAD), lambda b, i: (b, 0, i, 0))],
        out_shape=[jax.ShapeDtypeStruct((B, H_A, T, QK_PAD), BF16),
                   jax.ShapeDtypeStruct((B, H_A, T, V_PAD), BF16)],
        compiler_params=pltpu.CompilerParams(
            dimension_semantics=("arbitrary", "arbitrary"), vmem_limit_bytes=VMEM_LIMIT_BYTES),
        name="kvup",
    )(ckv, kr, prm["w_ukv"], prm["g_kn"])


HEADS_PER_STEP = 4


def _attn_kernel(safe_ref, q_ref, k_ref, v_ref, o_ref, acc_ref, m_ref, *, tq, wide, past):
    b, qi = pl.program_id(0), pl.program_id(2)
    diag0 = past + qi * tq
    heads = range(HEADS_PER_STEP)

    def chunk_mask(width):
        rows = lax.broadcasted_iota(jnp.int32, (tq, width), 0) // CHUNK
        cols = lax.broadcasted_iota(jnp.int32, (tq, width), 1) // CHUNK
        return cols <= rows + (width - tq) // CHUNK

    @pl.when(safe_ref[b] == 1)
    def _bounded():
        def step(ks, size, msk, first):
            scores = [_dot_nt(q_ref[0, h], k_ref[0, h, pl.ds(ks, size), :]) for h in heads]
            for h in heads:
                s = scores[h] if msk is None else jnp.where(msk, scores[h], NEG)
                pv = _dot(jnp.exp2(s).astype(BF16), v_ref[0, h, pl.ds(ks, size), :])
                if first:
                    acc_ref[h] = pv
                else:
                    acc_ref[h] += pv

        @pl.when(diag0 % wide == 0)
        def _():
            step(pl.multiple_of(diag0, tq), tq, chunk_mask(tq), True)

        @pl.when(diag0 % wide != 0)
        def _():
            step(pl.multiple_of(diag0 - tq, tq), 2 * tq, chunk_mask(2 * tq), True)

        @pl.loop(0, diag0 // wide)
        def _(j):
            step(pl.multiple_of(j * wide, wide), wide, None, False)

    @pl.when(safe_ref[b] == 0)
    def _online():
        acc_ref[...] = jnp.zeros_like(acc_ref)
        m_ref[...] = jnp.full_like(m_ref, NEG)

        def step(ks, msk):
            for h in heads:
                s = _dot_nt(q_ref[0, h], k_ref[0, h, pl.ds(ks, tq), :])
                if msk is not None:
                    s = jnp.where(msk, s, NEG)
                m_old = m_ref[h]
                m_new = jnp.maximum(m_old, jnp.max(s, axis=1, keepdims=True))
                p = jnp.exp2(s - m_new).astype(BF16)
                acc_ref[h] = (jnp.exp2(m_old - m_new) * acc_ref[h]
                              + _dot(p, v_ref[0, h, pl.ds(ks, tq), :]))
                m_ref[h] = m_new

        @pl.loop(0, diag0 // tq)
        def _(j):
            step(pl.multiple_of(j * tq, tq), None)

        step(pl.multiple_of(diag0, tq), chunk_mask(tq))

    for h in heads:
        acc = acc_ref[h]
        o_ref[0, :, h * V_DIM:(h + 1) * V_DIM] = (
            acc[:, :V_DIM] / acc[:, V_DIM:V_DIM + 1]).astype(BF16)


def _attention(q, k, v, safe, tq, wide):
    B, H, L, _ = q.shape
    T = k.shape[2]
    past = T - L
    hp = HEADS_PER_STEP
    assert past % wide == 0 and (wide == 2 * tq or L == tq) and L % tq == 0 and H % hp == 0
    grid_spec = pltpu.PrefetchScalarGridSpec(
        num_scalar_prefetch=1,
        grid=(B, H // hp, L // tq),
        in_specs=[pl.BlockSpec((1, hp, tq, QK_PAD), lambda b, h, i, s: (b, h, i, 0)),
                  pl.BlockSpec((1, hp, T, QK_PAD), lambda b, h, i, s: (b, h, 0, 0)),
                  pl.BlockSpec((1, hp, T, V_PAD), lambda b, h, i, s: (b, h, 0, 0))],
        out_specs=pl.BlockSpec((1, tq, hp * V_DIM), lambda b, h, i, s: (b, i, h)),
        scratch_shapes=[pltpu.VMEM((hp, tq, V_PAD), F32), pltpu.VMEM((hp, tq, 1), F32)],
    )
    return pl.pallas_call(
        functools.partial(_attn_kernel, tq=tq, wide=wide, past=past),
        grid_spec=grid_spec,
        out_shape=jax.ShapeDtypeStruct((B, L, D_ATT), BF16),
        compiler_params=pltpu.CompilerParams(
            dimension_semantics=("arbitrary", "arbitrary", "arbitrary"),
            vmem_limit_bytes=VMEM_LIMIT_BYTES),
        name="attention",
    )(safe, q, k, v)


def _mlstm_kernel(xcm_ref, vm_ref, gates_ref, og_ref, sgm_ref, o_ref, sza_ref, sga_ref, x_ref,
                  cbuf_ref, c0_ref, n0_ref, m0_ref, bif_ref, cw_ref, cb_ref, wq_ref, wkt_ref,
                  hg_ref, wpm_ref, wpa_ref, wout_ref, tri_ref,
                  y_ref, c_ref, n_ref, m_ref, xpad_ref, hn_ref, *, ch):
    ci = pl.program_id(1)

    @pl.when(ci == 0)
    def _():
        c_ref[...] = c0_ref[...]
        n_ref[...] = n0_ref[...]
        m_ref[...] = m0_ref[...]
        xpad_ref[0:SUBLANES, :] = cbuf_ref[0]

    pa = _dot(o_ref[0] * sza_ref[0], wpa_ref[...])

    xpad_ref[SUBLANES:, :] = xcm_ref[0].astype(F32)

    gpre = gates_ref[0] + bif_ref[...]
    lane = lax.broadcasted_iota(jnp.int32, (ch, LANES), 1)
    lf = jnp.minimum(gpre, 0.0) - jnp.log1p(jnp.exp(-jnp.abs(gpre)))
    lf = jnp.where((lane >= H_M) & (lane < 2 * H_M), lf, 0.0)
    tri = (lax.broadcasted_iota(jnp.int32, (ch, ch), 1)
           <= lax.broadcasted_iota(jnp.int32, (ch, ch), 0))
    bcum = sum(_dot(tri_ref[...], part) for part in _split3(lf))
    xmat = jnp.where(lane < H_M, gpre, -bcum)
    if ch % LANES:
        xmat = jnp.concatenate([xmat, jnp.zeros((LANES - ch % LANES, LANES), F32)], axis=0)
    xrow = xmat.T

    def project(hd):
        sl = slice(hd * DH_M, (hd + 1) * DH_M)
        u = cb_ref[:, sl] + xpad_ref[SUBLANES:, sl] * cw_ref[CONV_W - 1:CONV_W, sl]
        for j in range(CONV_W - 1):
            off = SUBLANES - (CONV_W - 1) + j
            u = u + xpad_ref[off:off + ch, sl] * cw_ref[j:j + 1, sl]
        su_h = (u * _sigmoid(u)).astype(BF16)
        qh = _dot(su_h, wq_ref[hd])
        kt = _dot_nt(wkt_ref[hd], su_h)
        qb = qh.astype(BF16)
        c0 = c_ref[0, hd]
        return qh, kt, _dot(qb, kt.astype(BF16)), _dot(qb, c0.astype(BF16)), c0

    def recur(hd, qh, kt, s_raw, q_c0, c0):
        sl = slice(hd * DH_M, (hd + 1) * DH_M)
        m0 = m_ref[0, hd:hd + 1, 0:1]
        bc = bcum[:, H_M + hd:H_M + hd + 1]
        drow = xrow[hd:hd + 1, :ch] + xrow[H_M + hd:H_M + hd + 1, :ch]
        dlog = jnp.where(tri, bc + drow, NEG)
        g = bc + m0
        m = jnp.maximum(g, jnp.max(dlog, axis=1, keepdims=True))
        w_intra = jnp.exp(dlog - m)
        w_inter = jnp.exp(g - m)

        v = vm_ref[0, :, sl]
        n0 = n_ref[0, hd:hd + 1, :]
        s = s_raw * w_intra
        num = w_inter * q_c0 + _dot(s.astype(BF16), v)
        den = (w_inter * jnp.sum(qh * n0, axis=1, keepdims=True)
               + jnp.sum(s, axis=1, keepdims=True))
        hh = num / jnp.maximum(jnp.abs(den), jnp.exp(-m))
        hn_ref[:, sl] = ((hh * _rms_scale(hh, DH_M)) * hg_ref[:, sl]).astype(BF16)

        m_last = m[ch - 1:ch, :]
        b_last = bc[ch - 1:ch, :]
        w_state = jnp.exp(drow + (b_last - m_last))
        decay = jnp.exp(b_last + m0 - m_last)
        c_ref[0, hd] = decay * c0 + _dot((kt * w_state).astype(BF16), v)
        w8 = jnp.broadcast_to(w_state, (SUBLANES, ch)).astype(BF16)
        n_ref[0, hd:hd + 1, :] = decay * n0 + _dot_nt(w8, kt.astype(BF16))[0:1, :]
        m_ref[0, hd:hd + 1, :] = jnp.broadcast_to(m_last, (1, LANES))

    pending = project(0)
    for hd in range(H_M):
        ahead = project(hd + 1) if hd + 1 < H_M else None
        recur(hd, *pending)
        pending = ahead
    xpad_ref[0:SUBLANES, :] = xpad_ref[ch:ch + SUBLANES, :]

    hm = hn_ref[...] * og_ref[0]
    pm = _dot(hm, wpm_ref[...])
    merged = sgm_ref[0] * pm.astype(BF16) + sga_ref[0] * pa.astype(BF16)
    y_ref[0] = x_ref[0] + _dot(merged, wout_ref[...])


def _mlstm(xcm, vm, gates, og, sgm, o, sza, sga, x, cbuf, c0, n0, m0b, prm, ch):
    B, L, _ = xcm.shape
    seq = lambda w: pl.BlockSpec((1, ch, w), lambda b, c: (b, c, 0))
    per_b = lambda *s: pl.BlockSpec((1,) + s, lambda b, c: (b,) + (0,) * len(s))
    consts = [prm["b_if"], prm["conv_w"], prm["conv_b"], prm["wq_m"], prm["wk_m"], prm["hnorm_g"],
              prm["w_pm"], prm["w_pa"], prm["w_out"],
              jnp.tril(jnp.ones((ch, ch), BF16))]
    return pl.pallas_call(
        functools.partial(_mlstm_kernel, ch=ch),
        grid=(B, L // ch),
        in_specs=[seq(D_MLSTM), seq(D_MLSTM), seq(LANES), seq(D_MLSTM), seq(D_MODEL),
                  seq(D_ATT), seq(D_ATT), seq(D_MODEL), seq(D_MODEL),
                  per_b(SUBLANES, D_MLSTM), per_b(H_M, DH_M, DH_M), per_b(H_M, DH_M),
                  per_b(H_M, LANES)] + [_const_spec(c.shape) for c in consts],
        out_specs=[seq(D_MODEL), per_b(H_M, DH_M, DH_M), per_b(H_M, DH_M), per_b(H_M, LANES)],
        out_shape=[jax.ShapeDtypeStruct((B, L, D_MODEL), F32),
                   jax.ShapeDtypeStruct((B, H_M, DH_M, DH_M), F32),
                   jax.ShapeDtypeStruct((B, H_M, DH_M), F32),
                   jax.ShapeDtypeStruct((B, H_M, LANES), F32)],
        scratch_shapes=[pltpu.VMEM((ch + SUBLANES, D_MLSTM), F32),
                        pltpu.VMEM((ch, D_MLSTM), BF16)],
        compiler_params=pltpu.CompilerParams(
            dimension_semantics=("arbitrary", "arbitrary"), vmem_limit_bytes=VMEM_LIMIT_BYTES),
        name="mlstm",
    )(xcm, vm, gates, og, sgm, o, sza, sga, x, cbuf, c0, n0, m0b, *consts)


def _swap_halves(a):
    half = ROPE // 2
    return jnp.concatenate([a[..., half:], a[..., :half]], axis=-1)


def _prepare_params(norm_g, w_in, b_if, conv_w, conv_b, wq_m, wk_m, hnorm_g, qn_g, w_uq, kvn_g,
                    w_ukv, g_qn, g_qr, g_kn, g_kr, w_pm, w_pa, w_out):
    cols, start = [], 0
    for w in SPLITS:
        cols.append(w_in[:, start:start + w])
        start += w
    xcm, vm, ig, fg, og, zm, cq, ckv, kr, za, gm, ga = cols
    pad = jnp.zeros((D_MODEL, LANES - 2 * H_M), F32)
    w_small = jnp.concatenate([ckv, kr, _swap_halves(kr), ig, fg, pad], axis=1)
    wuq = w_uq.reshape(Q_LORA, H_A, QK_DIM)
    wuq = jnp.concatenate([wuq, _swap_halves(wuq[..., NOPE:])], axis=-1)
    row = lambda a: a.reshape(1, -1).astype(F32)
    pair = lambda g: row(jnp.concatenate([g, _swap_halves(g)]))
    pair2 = lambda g: row(jnp.concatenate([_swap_halves(g), g]))
    return {
        "norm_g": row(norm_g),
        "w_big": jnp.concatenate([xcm, vm, og, zm, za, gm, ga], axis=1).astype(BF16),
        "w_cq": cq.astype(BF16),
        "w_small": w_small.astype(BF16),
        "qn_g": row(qn_g),
        "w_uq": wuq.reshape(Q_LORA, H_A * 2 * LANES).astype(BF16),
        "kvn_g": row(kvn_g),
        "g_kr": pair(g_kr), "g_kr2": pair2(g_kr),
        "g_qn": row(g_qn),
        "g_qr": pair(g_qr), "g_qr2": pair2(g_qr),
        "w_ukv": w_ukv.astype(BF16),
        "g_kn": row(g_kn),
        "q_bound2": (ATT_SCALE * LOG2E) ** 2 * (NOPE * jnp.max(g_qn * g_qn)
                                                + ROPE * jnp.max(g_qr * g_qr)),
        "kn_bound2": NOPE * jnp.max(g_kn * g_kn),
        "kr_bound2": ROPE * jnp.max(g_kr * g_kr),
        "b_if": row(jnp.concatenate([b_if, jnp.zeros((LANES - 2 * H_M,), F32)])),
        "conv_w": conv_w, "conv_b": row(conv_b),
        "wq_m": wq_m.astype(BF16),
        "wk_m": (jnp.swapaxes(wk_m, 1, 2) * (DH_M ** -0.5)).astype(BF16),
        "hnorm_g": row(hnorm_g),
        "w_pm": w_pm.astype(BF16), "w_pa": w_pa.astype(BF16), "w_out": w_out.astype(BF16),
    }


def _rope_tables(past, length):
    half = ROPE // 2
    inv = jnp.power(ROPE_THETA, -jnp.arange(half, dtype=F32) / half)
    ang = (past + jnp.arange(length)).astype(F32)[:, None] * inv[None, :]
    cos, sin = jnp.cos(ang), jnp.sin(ang)
    return jnp.tile(cos, (1, 4)), jnp.tile(jnp.concatenate([-sin, sin], axis=1), (1, 2))


def _hybrid_layer(x, past_ckv, past_kr, conv_buf, c0, n0, m0, prm, *, tm, tq, wide, ch):
    B, L, _ = x.shape
    past = past_ckv.shape[1]
    cos_t, sin_t = _rope_tables(past, L)
    seq = lambda a: a.reshape(B, L, a.shape[-1])

    kr2 = jnp.broadcast_to(prm["kr_bound2"], (B,))
    if past:
        pk = past_kr.astype(F32)
        kr2 = jnp.maximum(kr2, jnp.max(jnp.sum(pk * pk, axis=-1), axis=1))
    bound = jnp.sqrt((prm["kn_bound2"] + kr2 + 1.0) * prm["q_bound2"]) * 1.01
    safe = bound <= SAFE_BOUND
    negb = jnp.broadcast_to(jnp.where(safe, -bound, 0.0)[:, None, None], (B, 1, LANES))

    outs = _inproj(x, prm, cos_t, sin_t, negb, tm, emit_kv=not past)
    xcm, vm, og, sza, sgm, sga, gates, ckv_new, kr_new, q, tail = outs[:11]
    ckv_new, kr_new = seq(ckv_new), seq(kr_new)
    if past:
        T = past + L
        aug = jnp.concatenate(
            [jnp.ones((B, T, 1), F32), jnp.zeros((B, T, LANES - ROPE - 1), F32)], axis=-1)
        kr_all = jnp.concatenate([jnp.concatenate([past_kr, kr_new], axis=1), aug], axis=-1)
        k, v = _kvup(jnp.concatenate([past_ckv, ckv_new], axis=1), kr_all, prm, T)
    else:
        k, v = outs[11:]
    o = _attention(q, k, v, safe.astype(jnp.int32), tq, wide)

    cbuf = jnp.concatenate(
        [jnp.zeros((B, SUBLANES - (CONV_W - 1), D_MLSTM), F32), conv_buf.astype(F32)], axis=1)
    m0b = jnp.broadcast_to(m0.astype(F32)[:, :, None], (B, H_M, LANES))
    y, c1, n1, m1 = _mlstm(seq(xcm), seq(vm), seq(gates), seq(og), seq(sgm), o, seq(sza), seq(sga),
                           x, cbuf, c0.astype(F32), n0.astype(F32), m0b, prm, ch)
    conv_new = tail[:, SUBLANES - (CONV_W - 1):, :]
    return y, ckv_new, kr_new, conv_new, c1, n1, m1[:, :, 0]


def kernel(x_prompt, x_sample, cache_ckv, cache_kr, state_conv, state_C, state_n, state_m,
           norm_g, w_in, b_if, conv_w, conv_b, wq_m, wk_m, hnorm_g,
           qn_g, w_uq, kvn_g, w_ukv, g_qn, g_qr, g_kn, g_kr, w_pm, w_pa, w_out):
    dt = x_prompt.dtype
    B = x_prompt.shape[0]
    depth = norm_g.shape[0]
    yp, ys = x_prompt, x_sample
    outs_p = [[] for _ in range(6)]
    outs_s = [[] for _ in range(6)]
    for l in range(depth):
        prm = _prepare_params(norm_g[l], w_in[l], b_if[l], conv_w[l], conv_b[l], wq_m[l], wk_m[l],
                              hnorm_g[l], qn_g[l], w_uq[l], kvn_g[l], w_ukv[l], g_qn[l], g_qr[l],
                              g_kn[l], g_kr[l], w_pm[l], w_pa[l], w_out[l])
        yp, *res_p = _hybrid_layer(
            yp, jnp.zeros((B, 0, KV_LORA), dt), jnp.zeros((B, 0, ROPE), dt),
            jnp.zeros((B, CONV_W - 1, D_MLSTM), dt), jnp.zeros((B, H_M, DH_M, DH_M), F32),
            jnp.zeros((B, H_M, DH_M), F32), jnp.zeros((B, H_M), F32), prm,
            tm=512, tq=512, wide=1024, ch=256)
        Ls = x_sample.shape[1]
        ys, *res_s = _hybrid_layer(
            ys, cache_ckv[l], cache_kr[l], state_conv[l], state_C[l], state_n[l], state_m[l], prm,
            tm=Ls, tq=Ls, wide=cache_ckv.shape[2], ch=Ls)
        for lst, a in zip(outs_p, res_p):
            lst.append(a.astype(dt))
        for lst, a in zip(outs_s, res_s):
            lst.append(a.astype(dt))
    return (yp, ys) + tuple(jnp.stack(a) for a in outs_p) + tuple(jnp.stack(a) for a in outs_s)
```

```python
import functools
import math

import jax
import jax.numpy as jnp
from jax import lax
from jax.experimental import pallas as pl
from jax.experimental.pallas import tpu as pltpu

F32 = jnp.float32
BF16 = jnp.bfloat16

D_MODEL = 1024
CHUNK = 64
EPS = 1e-6
NEG = -1e30
D_MLSTM = D_MODEL
H_M = 4
DH_M = D_MLSTM // H_M
CONV_W = 4
H_A = 8
NOPE = 128
ROPE = 64
V_DIM = 128
Q_LORA = 384
KV_LORA = 256
D_ATT = H_A * V_DIM
ROPE_THETA = 10000.0
ATT_SCALE = (NOPE + ROPE) ** -0.5
LOG2E = math.log2(math.e)
QK_DIM = NOPE + ROPE
QK_PAD = 256
V_PAD = 256
SAFE_BOUND = 50.0
SPLITS = (D_MLSTM, D_MLSTM, H_M, H_M, D_MLSTM, D_MLSTM, Q_LORA, KV_LORA, ROPE, D_ATT, D_MODEL, D_MODEL)

LANES = 128
SUBLANES = 8
VMEM_LIMIT_BYTES = 56 * 1024 * 1024


def _const_spec(shape):
    nd = len(shape)
    return pl.BlockSpec(shape, lambda *_: (0,) * nd, pipeline_mode=pl.Buffered(1))


def _dot(a, b):
    return jnp.dot(a, b, preferred_element_type=F32)


def _dot_nt(a, b):
    return lax.dot_general(a, b, (((1,), (1,)), ((), ())), preferred_element_type=F32)


def _rms_scale(x, width):
    return lax.rsqrt(jnp.sum(x * x, axis=-1, keepdims=True) * (1.0 / width) + EPS)


def _sigmoid(x):
    return 0.5 * jnp.tanh(0.5 * x) + 0.5


def _split3(x):
    a = x.astype(BF16)
    r = x - a.astype(F32)
    b = r.astype(BF16)
    c = (r - b.astype(F32)).astype(BF16)
    return a, b, c


def _rope_half_norm(t, gain, gain_rolled, cos_t, sin_t):
    lane = lax.broadcasted_iota(jnp.int32, t.shape, 1)
    sq = jnp.where(lane < ROPE, t * t, 0.0)
    r = lax.rsqrt(jnp.sum(sq, axis=-1, keepdims=True) * (1.0 / ROPE) + EPS)
    t_sw = pltpu.roll(t, ROPE, axis=1)
    return r * (t * gain * cos_t + t_sw * gain_rolled * sin_t)


def _inproj_kernel(*refs, emit_kv):
    (x_ref, ng_ref, wbig_ref, wcq_ref, wsm_ref, qng_ref, wuq_ref, kvng_ref, gkr_ref, gkr2_ref,
     gqn_ref, gqr_ref, gqr2_ref, wukv_ref, gkn_ref, cos_ref, sin_ref, negb_ref,
     xcm_ref, vm_ref, og_ref, sza_ref, sgm_ref, sga_ref, gates_ref, ckv_ref, kr_ref, q_ref,
     tail_ref, *kv_refs) = refs
    x = x_ref[...]
    h = ((x * _rms_scale(x, D_MODEL)) * ng_ref[...]).astype(BF16)
    cos_t = cos_ref[...]
    sin_t = sin_ref[...]
    lane = lax.broadcasted_iota(jnp.int32, (x.shape[0], LANES), 1)

    def seg(i):
        return _dot(h, wbig_ref[:, i * D_MODEL:(i + 1) * D_MODEL])

    cq = _dot(h, wcq_ref[...])
    cqn = ((cq * _rms_scale(cq, Q_LORA)) * qng_ref[...]).astype(BF16)
    qscale = ATT_SCALE * LOG2E
    q_aug = jnp.where(lane == ROPE, negb_ref[0], 0.0)

    def q_head(hd):
        qh = _dot(cqn, wuq_ref[:, hd * 2 * LANES:(hd + 1) * 2 * LANES])
        qn = qh[:, :NOPE]
        qn = (qn * _rms_scale(qn, NOPE)) * gqn_ref[...]
        q_ref[0, hd, :, :NOPE] = (qn * qscale).astype(BF16)
        qr = _rope_half_norm(qh[:, NOPE:], gqr_ref[...], gqr2_ref[...], cos_t, sin_t)
        q_ref[0, hd, :, NOPE:] = jnp.where(lane < ROPE, qr * qscale, q_aug).astype(BF16)

    t = _dot(h, wsm_ref[...])
    c = t[:, :KV_LORA]
    ckv = (c * _rms_scale(c, KV_LORA)) * kvng_ref[...]
    ckv_ref[...] = ckv
    kr = _rope_half_norm(t[:, KV_LORA:KV_LORA + LANES], gkr_ref[...], gkr2_ref[...], cos_t, sin_t)
    kr_ref[...] = kr[:, :ROPE]
    gates_ref[...] = t[:, KV_LORA + LANES:]

    if emit_kv:
        k_ref, v_ref = kv_refs
        ckv_b = ckv.astype(BF16)
        kr_aug = jnp.where(lane < ROPE, kr, (lane == ROPE).astype(F32)).astype(BF16)

        def kv_head(hd):
            _kv_head(hd, ckv_b, kr_aug, wukv_ref, gkn_ref, k_ref, v_ref)
    else:
        def kv_head(hd):
            pass

    q_head(0)
    kv_head(0)
    z = seg(3)
    q_head(1)
    kv_head(1)
    og_ref[...] = (_sigmoid(seg(2)) * (z * _sigmoid(z))).astype(BF16)
    q_head(2)
    kv_head(2)
    z = seg(4)
    sza_ref[...] = (z * _sigmoid(z)).astype(BF16)
    q_head(3)
    kv_head(3)
    sgm_ref[...] = _sigmoid(seg(5)).astype(BF16)
    q_head(4)
    kv_head(4)
    sga_ref[...] = _sigmoid(seg(6)).astype(BF16)
    q_head(5)
    kv_head(5)
    q_head(6)
    kv_head(6)
    q_head(7)
    kv_head(7)
    xcm = seg(0)
    xcm_ref[...] = xcm.astype(BF16)
    tail_ref[0] = xcm[xcm.shape[0] - SUBLANES:, :]
    vm_ref[...] = seg(1).astype(BF16)


def _kv_head(hd, ckv_b, kr_aug, w_ref, gkn_ref, k_ref, v_ref):
    kv = _dot(ckv_b, w_ref[:, hd * 2 * LANES:(hd + 1) * 2 * LANES])
    kn = kv[:, :NOPE]
    k_ref[0, hd, :, :NOPE] = ((kn * _rms_scale(kn, NOPE)) * gkn_ref[...]).astype(BF16)
    k_ref[0, hd, :, NOPE:] = kr_aug
    lane = lax.broadcasted_iota(jnp.int32, kv.shape, 1)
    v_aug = jnp.where(lane < V_DIM, pltpu.roll(kv, V_DIM, axis=1), (lane == V_DIM).astype(F32))
    v_ref[0, hd] = v_aug.astype(BF16)


def _inproj(x, prm, cos_t, sin_t, negb, tm, emit_kv):
    B, L, _ = x.shape
    T = B * L
    tps = L // tm
    x2 = x.reshape(T, D_MODEL)
    row = lambda w: pl.BlockSpec((tm, w), lambda i: (i, 0))
    pos = pl.BlockSpec((tm, LANES), lambda i: (i % tps, 0))
    head_rows = lambda w: pl.BlockSpec((1, H_A, tm, w), lambda i: (i // tps, 0, i % tps, 0))
    consts = [prm["norm_g"], prm["w_big"], prm["w_cq"], prm["w_small"], prm["qn_g"], prm["w_uq"],
              prm["kvn_g"], prm["g_kr"], prm["g_kr2"], prm["g_qn"], prm["g_qr"], prm["g_qr2"],
              prm["w_ukv"], prm["g_kn"]]
    wide = jax.ShapeDtypeStruct((T, D_MODEL), BF16)
    heads = lambda w: jax.ShapeDtypeStruct((B, H_A, L, w), BF16)
    out_shape = [wide] * 6 + [
        jax.ShapeDtypeStruct((T, LANES), F32),
        jax.ShapeDtypeStruct((T, KV_LORA), F32),
        jax.ShapeDtypeStruct((T, ROPE), F32),
        heads(QK_PAD),
        jax.ShapeDtypeStruct((B, SUBLANES, D_MLSTM), F32),
    ] + ([heads(QK_PAD), heads(V_PAD)] if emit_kv else [])
    out_specs = [row(D_MODEL)] * 6 + [
        row(LANES), row(KV_LORA), row(ROPE), head_rows(QK_PAD),
        pl.BlockSpec((1, SUBLANES, D_MLSTM), lambda i: (i // tps, 0, 0)),
    ] + ([head_rows(QK_PAD), head_rows(V_PAD)] if emit_kv else [])
    return pl.pallas_call(
        functools.partial(_inproj_kernel, emit_kv=emit_kv),
        grid=(T // tm,),
        in_specs=[row(D_MODEL)] + [_const_spec(c.shape) for c in consts] + [
            pos, pos, pl.BlockSpec((1, 1, LANES), lambda i: (i // tps, 0, 0))],
        out_specs=out_specs,
        out_shape=out_shape,
        compiler_params=pltpu.CompilerParams(
            dimension_semantics=("arbitrary",), vmem_limit_bytes=VMEM_LIMIT_BYTES),
        name="inproj",
    )(x2, *consts, cos_t, sin_t, negb)


def _kvup_kernel(ckv_ref, kr_ref, w_ref, gkn_ref, k_ref, v_ref):
    ckv_b = ckv_ref[0].astype(BF16)
    kr_aug = kr_ref[0].astype(BF16)
    for hd in range(H_A):
        _kv_head(hd, ckv_b, kr_aug, w_ref, gkn_ref, k_ref, v_ref)


def _kvup(ckv, kr, prm, tm):
    B, T, _ = ckv.shape
    return pl.pallas_call(
        _kvup_kernel,
        grid=(B, T // tm),
        in_specs=[pl.BlockSpec((1, tm, KV_LORA), lambda b, i: (b, i, 0)),
                  pl.BlockSpec((1, tm, LANES), lambda b, i: (b, i, 0)),
                  _const_spec(prm["w_ukv"].shape), _const_spec(prm["g_kn"].shape)],
        out_specs=[pl.BlockSpec((1, H_A, tm, QK_PAD), lambda b, i: (b, 0, i, 0)),
                   pl.BlockSpec((1, H_A, tm, V_PAD), lambda b, i: (b, 0, i, 0))],
        out_shape=[jax.ShapeDtypeStruct((B, H_A, T, QK_PAD), BF16),
                   jax.ShapeDtypeStruct((B, H_A, T, V_PAD), BF16)],
        compiler_params=pltpu.CompilerParams(
            dimension_semantics=("arbitrary", "arbitrary"), vmem_limit_bytes=VMEM_LIMIT_BYTES),
        name="kvup",
    )(ckv, kr, prm["w_ukv"], prm["g_kn"])


HEADS_PER_STEP = 4


def _attn_kernel(safe_ref, q_ref, k_ref, v_ref, o_ref, acc_ref, m_ref, *, tq, wide, past):
    b, qi = pl.program_id(0), pl.program_id(2)
    diag0 = past + qi * tq
    heads = range(HEADS_PER_STEP)

    def chunk_mask(width):
        rows = lax.broadcasted_iota(jnp.int32, (tq, width), 0) // CHUNK
        cols = lax.broadcasted_iota(jnp.int32, (tq, width), 1) // CHUNK
        return cols <= rows + (width - tq) // CHUNK

    @pl.when(safe_ref[b] == 1)
    def _bounded():
        def step(ks, size, msk, first):
            scores = [_dot_nt(q_ref[0, h], k_ref[0, h, pl.ds(ks, size), :]) for h in heads]
            for h in heads:
                s = scores[h] if msk is None else jnp.where(msk, scores[h], NEG)
                pv = _dot(jnp.exp2(s).astype(BF16), v_ref[0, h, pl.ds(ks, size), :])
                if first:
                    acc_ref[h] = pv
                else:
                    acc_ref[h] += pv

        def diag_step(ks, before):
            if tq % (2 * CHUNK):
                step(ks, before + tq, chunk_mask(before + tq), True)
                return
            half = tq // 2
            for r0, nkeys in ((0, before + half), (half, before + tq)):
                rows = slice(r0, r0 + half)
                rr = lax.broadcasted_iota(jnp.int32, (half, nkeys), 0) // CHUNK
                cc = lax.broadcasted_iota(jnp.int32, (half, nkeys), 1) // CHUNK
                msk = cc <= rr + (before + r0) // CHUNK
                scores = [_dot_nt(q_ref[0, h, rows, :], k_ref[0, h, pl.ds(ks, nkeys), :])
                          for h in heads]
                for h in heads:
                    p = jnp.exp2(jnp.where(msk, scores[h], NEG)).astype(BF16)
                    acc_ref[h, rows, :] = _dot(p, v_ref[0, h, pl.ds(ks, nkeys), :])

        @pl.when(diag0 % wide == 0)
        def _():
            diag_step(pl.multiple_of(diag0, tq), 0)

        @pl.when(diag0 % wide != 0)
        def _():
            diag_step(pl.multiple_of(diag0 - tq, tq), tq)

        @pl.loop(0, diag0 // wide)
        def _(j):
            step(pl.multiple_of(j * wide, wide), wide, None, False)

    @pl.when(safe_ref[b] == 0)
    def _online():
        acc_ref[...] = jnp.zeros_like(acc_ref)
        m_ref[...] = jnp.full_like(m_ref, NEG)

        def step(ks, msk):
            for h in heads:
                s = _dot_nt(q_ref[0, h], k_ref[0, h, pl.ds(ks, tq), :])
                if msk is not None:
                    s = jnp.where(msk, s, NEG)
                m_old = m_ref[h]
                m_new = jnp.maximum(m_old, jnp.max(s, axis=1, keepdims=True))
                p = jnp.exp2(s - m_new).astype(BF16)
                acc_ref[h] = (jnp.exp2(m_old - m_new) * acc_ref[h]
                              + _dot(p, v_ref[0, h, pl.ds(ks, tq), :]))
                m_ref[h] = m_new

        @pl.loop(0, diag0 // tq)
        def _(j):
            step(pl.multiple_of(j * tq, tq), None)

        step(pl.multiple_of(diag0, tq), chunk_mask(tq))

    for h in heads:
        acc = acc_ref[h]
        o_ref[0, :, h * V_DIM:(h + 1) * V_DIM] = (
            acc[:, :V_DIM] / acc[:, V_DIM:V_DIM + 1]).astype(BF16)


def _attention(q, k, v, safe, tq, wide):
    B, H, L, _ = q.shape
    T = k.shape[2]
    past = T - L
    hp = HEADS_PER_STEP
    assert past % wide == 0 and (wide == 2 * tq or L == tq) and L % tq == 0 and H % hp == 0
    grid_spec = pltpu.PrefetchScalarGridSpec(
        num_scalar_prefetch=1,
        grid=(B, H // hp, L // tq),
        in_specs=[pl.BlockSpec((1, hp, tq, QK_PAD), lambda b, h, i, s: (b, h, i, 0)),
                  pl.BlockSpec((1, hp, T, QK_PAD), lambda b, h, i, s: (b, h, 0, 0)),
                  pl.BlockSpec((1, hp, T, V_PAD), lambda b, h, i, s: (b, h, 0, 0))],
        out_specs=pl.BlockSpec((1, tq, hp * V_DIM), lambda b, h, i, s: (b, i, h)),
        scratch_shapes=[pltpu.VMEM((hp, tq, V_PAD), F32), pltpu.VMEM((hp, tq, 1), F32)],
    )
    return pl.pallas_call(
        functools.partial(_attn_kernel, tq=tq, wide=wide, past=past),
        grid_spec=grid_spec,
        out_shape=jax.ShapeDtypeStruct((B, L, D_ATT), BF16),
        compiler_params=pltpu.CompilerParams(
            dimension_semantics=("arbitrary", "arbitrary", "arbitrary"),
            vmem_limit_bytes=VMEM_LIMIT_BYTES),
        name="attention",
    )(safe, q, k, v)


def _mlstm_kernel(xcm_ref, vm_ref, gates_ref, og_ref, sgm_ref, o_ref, sza_ref, sga_ref, x_ref,
                  cbuf_ref, c0_ref, n0_ref, m0_ref, bif_ref, cw_ref, cb_ref, wq_ref, wkt_ref,
                  hg_ref, wpm_ref, wpa_ref, wout_ref, tri_ref,
                  y_ref, c_ref, n_ref, m_ref, xpad_ref, hn_ref, *, ch):
    ci = pl.program_id(1)

    @pl.when(ci == 0)
    def _():
        c_ref[...] = c0_ref[...]
        n_ref[...] = n0_ref[...]
        m_ref[...] = m0_ref[...]
        xpad_ref[0:SUBLANES, :] = cbuf_ref[0]

    pa = _dot(o_ref[0] * sza_ref[0], wpa_ref[...])

    xpad_ref[SUBLANES:, :] = xcm_ref[0].astype(F32)

    gpre = gates_ref[0] + bif_ref[...]
    lane = lax.broadcasted_iota(jnp.int32, (ch, LANES), 1)
    lf = jnp.minimum(gpre, 0.0) - jnp.log1p(jnp.exp(-jnp.abs(gpre)))
    lf = jnp.where((lane >= H_M) & (lane < 2 * H_M), lf, 0.0)
    tri = (lax.broadcasted_iota(jnp.int32, (ch, ch), 1)
           <= lax.broadcasted_iota(jnp.int32, (ch, ch), 0))
    bcum = sum(_dot(tri_ref[...], part) for part in _split3(lf))
    xmat = jnp.where(lane < H_M, gpre, -bcum)
    if ch % LANES:
        xmat = jnp.concatenate([xmat, jnp.zeros((LANES - ch % LANES, LANES), F32)], axis=0)
    xrow = xmat.T

    def project(hd):
        sl = slice(hd * DH_M, (hd + 1) * DH_M)
        u = cb_ref[:, sl] + xpad_ref[SUBLANES:, sl] * cw_ref[CONV_W - 1:CONV_W, sl]
        for j in range(CONV_W - 1):
            off = SUBLANES - (CONV_W - 1) + j
            u = u + xpad_ref[off:off + ch, sl] * cw_ref[j:j + 1, sl]
        su_h = (u * _sigmoid(u)).astype(BF16)
        qh = _dot(su_h, wq_ref[hd])
        kt = _dot_nt(wkt_ref[hd], su_h)
        qb = qh.astype(BF16)
        c0 = c_ref[0, hd]
        return qh, kt, _dot(qb, kt.astype(BF16)), _dot(qb, c0.astype(BF16)), c0

    def recur(hd, qh, kt, s_raw, q_c0, c0):
        sl = slice(hd * DH_M, (hd + 1) * DH_M)
        m0 = m_ref[0, hd:hd + 1, 0:1]
        bc = bcum[:, H_M + hd:H_M + hd + 1]
        drow = xrow[hd:hd + 1, :ch] + xrow[H_M + hd:H_M + hd + 1, :ch]
        dlog = jnp.where(tri, bc + drow, NEG)
        g = bc + m0
        m = jnp.maximum(g, jnp.max(dlog, axis=1, keepdims=True))
        w_intra = jnp.exp(dlog - m)
        w_inter = jnp.exp(g - m)

        v = vm_ref[0, :, sl]
        n0 = n_ref[0, hd:hd + 1, :]
        s = s_raw * w_intra
        num = w_inter * q_c0 + _dot(s.astype(BF16), v)
        den = (w_inter * jnp.sum(qh * n0, axis=1, keepdims=True)
               + jnp.sum(s, axis=1, keepdims=True))
        hh = num / jnp.maximum(jnp.abs(den), jnp.exp(-m))
        hn_ref[:, sl] = ((hh * _rms_scale(hh, DH_M)) * hg_ref[:, sl]).astype(BF16)

        m_last = m[ch - 1:ch, :]
        b_last = bc[ch - 1:ch, :]
        w_state = jnp.exp(drow + (b_last - m_last))
        decay = jnp.exp(b_last + m0 - m_last)
        c_ref[0, hd] = decay * c0 + _dot((kt * w_state).astype(BF16), v)
        w8 = jnp.broadcast_to(w_state, (SUBLANES, ch)).astype(BF16)
        n_ref[0, hd:hd + 1, :] = decay * n0 + _dot_nt(w8, kt.astype(BF16))[0:1, :]
        m_ref[0, hd:hd + 1, :] = jnp.broadcast_to(m_last, (1, LANES))

    pending = project(0)
    for hd in range(H_M):
        ahead = project(hd + 1) if hd + 1 < H_M else None
        recur(hd, *pending)
        pending = ahead
    xpad_ref[0:SUBLANES, :] = xpad_ref[ch:ch + SUBLANES, :]

    hm = hn_ref[...] * og_ref[0]
    pm = _dot(hm, wpm_ref[...])
    merged = sgm_ref[0] * pm.astype(BF16) + sga_ref[0] * pa.astype(BF16)
    y_ref[0] = x_ref[0] + _dot(merged, wout_ref[...])


def _mlstm(xcm, vm, gates, og, sgm, o, sza, sga, x, cbuf, c0, n0, m0b, prm, ch):
    B, L, _ = xcm.shape
    seq = lambda w: pl.BlockSpec((1, ch, w), lambda b, c: (b, c, 0))
    per_b = lambda *s: pl.BlockSpec((1,) + s, lambda b, c: (b,) + (0,) * len(s))
    consts = [prm["b_if"], prm["conv_w"], prm["conv_b"], prm["wq_m"], prm["wk_m"], prm["hnorm_g"],
              prm["w_pm"], prm["w_pa"], prm["w_out"],
              jnp.tril(jnp.ones((ch, ch), BF16))]
    return pl.pallas_call(
        functools.partial(_mlstm_kernel, ch=ch),
        grid=(B, L // ch),
        in_specs=[seq(D_MLSTM), seq(D_MLSTM), seq(LANES), seq(D_MLSTM), seq(D_MODEL),
                  seq(D_ATT), seq(D_ATT), seq(D_MODEL), seq(D_MODEL),
                  per_b(SUBLANES, D_MLSTM), per_b(H_M, DH_M, DH_M), per_b(H_M, DH_M),
                  per_b(H_M, LANES)] + [_const_spec(c.shape) for c in consts],
        out_specs=[seq(D_MODEL), per_b(H_M, DH_M, DH_M), per_b(H_M, DH_M), per_b(H_M, LANES)],
        out_shape=[jax.ShapeDtypeStruct((B, L, D_MODEL), F32),
                   jax.ShapeDtypeStruct((B, H_M, DH_M, DH_M), F32),
                   jax.ShapeDtypeStruct((B, H_M, DH_M), F32),
                   jax.ShapeDtypeStruct((B, H_M, LANES), F32)],
        scratch_shapes=[pltpu.VMEM((ch + SUBLANES, D_MLSTM), F32),
                        pltpu.VMEM((ch, D_MLSTM), BF16)],
        compiler_params=pltpu.CompilerParams(
            dimension_semantics=("arbitrary", "arbitrary"), vmem_limit_bytes=VMEM_LIMIT_BYTES),
        name="mlstm",
    )(xcm, vm, gates, og, sgm, o, sza, sga, x, cbuf, c0, n0, m0b, *consts)


def _swap_halves(a):
    half = ROPE // 2
    return jnp.concatenate([a[..., half:], a[..., :half]], axis=-1)


def _prepare_params(norm_g, w_in, b_if, conv_w, conv_b, wq_m, wk_m, hnorm_g, qn_g, w_uq, kvn_g,
                    w_ukv, g_qn, g_qr, g_kn, g_kr, w_pm, w_pa, w_out):
    cols, start = [], 0
    for w in SPLITS:
        cols.append(w_in[:, start:start + w])
        start += w
    xcm, vm, ig, fg, og, zm, cq, ckv, kr, za, gm, ga = cols
    pad = jnp.zeros((D_MODEL, LANES - 2 * H_M), F32)
    w_small = jnp.concatenate([ckv, kr, _swap_halves(kr), ig, fg, pad], axis=1)
    wuq = w_uq.reshape(Q_LORA, H_A, QK_DIM)
    wuq = jnp.concatenate([wuq, _swap_halves(wuq[..., NOPE:])], axis=-1)
    row = lambda a: a.reshape(1, -1).astype(F32)
    pair = lambda g: row(jnp.concatenate([g, _swap_halves(g)]))
    pair2 = lambda g: row(jnp.concatenate([_swap_halves(g), g]))
    return {
        "norm_g": row(norm_g),
        "w_big": jnp.concatenate([xcm, vm, og, zm, za, gm, ga], axis=1).astype(BF16),
        "w_cq": cq.astype(BF16),
        "w_small": w_small.astype(BF16),
        "qn_g": row(qn_g),
        "w_uq": wuq.reshape(Q_LORA, H_A * 2 * LANES).astype(BF16),
        "kvn_g": row(kvn_g),
        "g_kr": pair(g_kr), "g_kr2": pair2(g_kr),
        "g_qn": row(g_qn),
        "g_qr": pair(g_qr), "g_qr2": pair2(g_qr),
        "w_ukv": w_ukv.astype(BF16),
        "g_kn": row(g_kn),
        "q_bound2": (ATT_SCALE * LOG2E) ** 2 * (NOPE * jnp.max(g_qn * g_qn)
                                                + ROPE * jnp.max(g_qr * g_qr)),
        "kn_bound2": NOPE * jnp.max(g_kn * g_kn),
        "kr_bound2": ROPE * jnp.max(g_kr * g_kr),
        "b_if": row(jnp.concatenate([b_if, jnp.zeros((LANES - 2 * H_M,), F32)])),
        "conv_w": conv_w, "conv_b": row(conv_b),
        "wq_m": wq_m.astype(BF16),
        "wk_m": (jnp.swapaxes(wk_m, 1, 2) * (DH_M ** -0.5)).astype(BF16),
        "hnorm_g": row(hnorm_g),
        "w_pm": w_pm.astype(BF16), "w_pa": w_pa.astype(BF16), "w_out": w_out.astype(BF16),
    }


def _rope_tables(past, length):
    half = ROPE // 2
    inv = jnp.power(ROPE_THETA, -jnp.arange(half, dtype=F32) / half)
    ang = (past + jnp.arange(length)).astype(F32)[:, None] * inv[None, :]
    cos, sin = jnp.cos(ang), jnp.sin(ang)
    return jnp.tile(cos, (1, 4)), jnp.tile(jnp.concatenate([-sin, sin], axis=1), (1, 2))


def _hybrid_layer(x, past_ckv, past_kr, conv_buf, c0, n0, m0, prm, *, tm, tq, wide, ch):
    B, L, _ = x.shape
    past = past_ckv.shape[1]
    cos_t, sin_t = _rope_tables(past, L)
    seq = lambda a: a.reshape(B, L, a.shape[-1])

    kr2 = jnp.broadcast_to(prm["kr_bound2"], (B,))
    if past:
        pk = past_kr.astype(F32)
        kr2 = jnp.maximum(kr2, jnp.max(jnp.sum(pk * pk, axis=-1), axis=1))
    bound = jnp.sqrt((prm["kn_bound2"] + kr2 + 1.0) * prm["q_bound2"]) * 1.01
    safe = bound <= SAFE_BOUND
    negb = jnp.broadcast_to(jnp.where(safe, -bound, 0.0)[:, None, None], (B, 1, LANES))

    outs = _inproj(x, prm, cos_t, sin_t, negb, tm, emit_kv=not past)
    xcm, vm, og, sza, sgm, sga, gates, ckv_new, kr_new, q, tail = outs[:11]
    ckv_new, kr_new = seq(ckv_new), seq(kr_new)
    if past:
        T = past + L
        aug = jnp.concatenate(
            [jnp.ones((B, T, 1), F32), jnp.zeros((B, T, LANES - ROPE - 1), F32)], axis=-1)
        kr_all = jnp.concatenate([jnp.concatenate([past_kr, kr_new], axis=1), aug], axis=-1)
        k, v = _kvup(jnp.concatenate([past_ckv, ckv_new], axis=1), kr_all, prm, T)
    else:
        k, v = outs[11:]
    o = _attention(q, k, v, safe.astype(jnp.int32), tq, wide)

    cbuf = jnp.concatenate(
        [jnp.zeros((B, SUBLANES - (CONV_W - 1), D_MLSTM), F32), conv_buf.astype(F32)], axis=1)
    m0b = jnp.broadcast_to(m0.astype(F32)[:, :, None], (B, H_M, LANES))
    y, c1, n1, m1 = _mlstm(seq(xcm), seq(vm), seq(gates), seq(og), seq(sgm), o, seq(sza), seq(sga),
                           x, cbuf, c0.astype(F32), n0.astype(F32), m0b, prm, ch)
    conv_new = tail[:, SUBLANES - (CONV_W - 1):, :]
    return y, ckv_new, kr_new, conv_new, c1, n1, m1[:, :, 0]


def kernel(x_prompt, x_sample, cache_ckv, cache_kr, state_conv, state_C, state_n, state_m,
           norm_g, w_in, b_if, conv_w, conv_b, wq_m, wk_m, hnorm_g,
           qn_g, w_uq, kvn_g, w_ukv, g_qn, g_qr, g_kn, g_kr, w_pm, w_pa, w_out):
    dt = x_prompt.dtype
    B = x_prompt.shape[0]
    depth = norm_g.shape[0]
    yp, ys = x_prompt, x_sample
    outs_p = [[] for _ in range(6)]
    outs_s = [[] for _ in range(6)]
    for l in range(depth):
        prm = _prepare_params(norm_g[l], w_in[l], b_if[l], conv_w[l], conv_b[l], wq_m[l], wk_m[l],
                              hnorm_g[l], qn_g[l], w_uq[l], kvn_g[l], w_ukv[l], g_qn[l], g_qr[l],
                              g_kn[l], g_kr[l], w_pm[l], w_pa[l], w_out[l])
        yp, *res_p = _hybrid_layer(
            yp, jnp.zeros((B, 0, KV_LORA), dt), jnp.zeros((B, 0, ROPE), dt),
            jnp.zeros((B, CONV_W - 1, D_MLSTM), dt), jnp.zeros((B, H_M, DH_M, DH_M), F32),
            jnp.zeros((B, H_M, DH_M), F32), jnp.zeros((B, H_M), F32), prm,
            tm=512, tq=512, wide=1024, ch=256)
        Ls = x_sample.shape[1]
        ys, *res_s = _hybrid_layer(
            ys, cache_ckv[l], cache_kr[l], state_conv[l], state_C[l], state_n[l], state_m[l], prm,
            tm=Ls, tq=Ls, wide=cache_ckv.shape[2], ch=Ls)
        for lst, a in zip(outs_p, res_p):
            lst.append(a.astype(dt))
        for lst, a in zip(outs_s, res_s):
            lst.append(a.astype(dt))
    return (yp, ys) + tuple(jnp.stack(a) for a in outs_p) + tuple(jnp.stack(a) for a in outs_s)
```

```python
import functools
import math

import jax
import jax.numpy as jnp
from jax import lax
from jax.experimental import pallas as pl
from jax.experimental.pallas import tpu as pltpu

F32 = jnp.float32
BF16 = jnp.bfloat16

D_MODEL = 1024
CHUNK = 64
EPS = 1e-6
NEG = -1e30
D_MLSTM = D_MODEL
H_M = 4
DH_M = D_MLSTM // H_M
CONV_W = 4
H_A = 8
NOPE = 128
ROPE = 64
V_DIM = 128
Q_LORA = 384
KV_LORA = 256
D_ATT = H_A * V_DIM
ROPE_THETA = 10000.0
ATT_SCALE = (NOPE + ROPE) ** -0.5
LOG2E = math.log2(math.e)
QK_DIM = NOPE + ROPE
QK_PAD = 256
V_PAD = 256
N_WIDE = 7
SAFE_BOUND = 50.0
SPLITS = (D_MLSTM, D_MLSTM, H_M, H_M, D_MLSTM, D_MLSTM, Q_LORA, KV_LORA, ROPE, D_ATT, D_MODEL, D_MODEL)

LANES = 128
SUBLANES = 8
VMEM_LIMIT_BYTES = 56 * 1024 * 1024


def _const_spec(shape):
    nd = len(shape)
    return pl.BlockSpec(shape, lambda *_: (0,) * nd, pipeline_mode=pl.Buffered(1))


def _dot(a, b):
    return jnp.dot(a, b, preferred_element_type=F32)


def _dot_nt(a, b):
    return lax.dot_general(a, b, (((1,), (1,)), ((), ())), preferred_element_type=F32)


def _rms_scale(x, width):
    return lax.rsqrt(jnp.sum(x * x, axis=-1, keepdims=True) * (1.0 / width) + EPS)


def _sigmoid(x):
    return 0.5 * jnp.tanh(0.5 * x) + 0.5


def _split3(x):
    a = x.astype(BF16)
    r = x - a.astype(F32)
    b = r.astype(BF16)
    c = (r - b.astype(F32)).astype(BF16)
    return a, b, c


def _rope_half_norm(t, gain, gain_rolled, cos_t, sin_t):
    lane = lax.broadcasted_iota(jnp.int32, t.shape, 1)
    sq = jnp.where(lane < ROPE, t * t, 0.0)
    r = lax.rsqrt(jnp.sum(sq, axis=-1, keepdims=True) * (1.0 / ROPE) + EPS)
    t_sw = pltpu.roll(t, ROPE, axis=1)
    return r * (t * gain * cos_t + t_sw * gain_rolled * sin_t)


def _inproj_kernel(*refs, emit_kv):
    (x_ref, ng_ref, wcq_ref, wsm_ref, qng_ref, wuq_ref, kvng_ref, gkr_ref, gkr2_ref,
     gqn_ref, gqr_ref, gqr2_ref, wukv_ref, gkn_ref, *refs) = refs
    wseg_refs, refs = refs[:N_WIDE], refs[N_WIDE:]
    (cos_ref, sin_ref, negb_ref,
     xcm_ref, vm_ref, og_ref, sza_ref, sgm_ref, sga_ref, gates_ref, ckv_ref, kr_ref, q_ref,
     tail_ref, *kv_refs) = refs
    x = x_ref[...]
    h = ((x * _rms_scale(x, D_MODEL)) * ng_ref[...]).astype(BF16)
    cos_t = cos_ref[...]
    sin_t = sin_ref[...]
    lane = lax.broadcasted_iota(jnp.int32, (x.shape[0], LANES), 1)

    def seg(i):
        return _dot(h, wseg_refs[i][...])

    cq = _dot(h, wcq_ref[...])
    cqn = ((cq * _rms_scale(cq, Q_LORA)) * qng_ref[...]).astype(BF16)
    qscale = ATT_SCALE * LOG2E
    q_aug = jnp.where(lane == ROPE, negb_ref[0], 0.0)

    def q_head(hd):
        qh = _dot(cqn, wuq_ref[:, hd * 2 * LANES:(hd + 1) * 2 * LANES])
        qn = qh[:, :NOPE]
        qn = (qn * _rms_scale(qn, NOPE)) * gqn_ref[...]
        q_ref[0, hd, :, :NOPE] = (qn * qscale).astype(BF16)
        qr = _rope_half_norm(qh[:, NOPE:], gqr_ref[...], gqr2_ref[...], cos_t, sin_t)
        q_ref[0, hd, :, NOPE:] = jnp.where(lane < ROPE, qr * qscale, q_aug).astype(BF16)

    t = _dot(h, wsm_ref[...])
    c = t[:, :KV_LORA]
    ckv = (c * _rms_scale(c, KV_LORA)) * kvng_ref[...]
    ckv_ref[...] = ckv
    kr = _rope_half_norm(t[:, KV_LORA:KV_LORA + LANES], gkr_ref[...], gkr2_ref[...], cos_t, sin_t)
    kr_ref[...] = kr[:, :ROPE]
    gates_ref[...] = t[:, KV_LORA + LANES:]

    if emit_kv:
        k_ref, v_ref = kv_refs
        ckv_b = ckv.astype(BF16)
        kr_aug = jnp.where(lane < ROPE, kr, (lane == ROPE).astype(F32)).astype(BF16)

        def kv_head(hd):
            _kv_head(hd, ckv_b, kr_aug, wukv_ref, gkn_ref, k_ref, v_ref)
    else:
        def kv_head(hd):
            pass

    q_head(0)
    kv_head(0)
    z = seg(3)
    q_head(1)
    kv_head(1)
    og_ref[...] = (_sigmoid(seg(2)) * (z * _sigmoid(z))).astype(BF16)
    q_head(2)
    kv_head(2)
    z = seg(4)
    sza_ref[...] = (z * _sigmoid(z)).astype(BF16)
    q_head(3)
    kv_head(3)
    sgm_ref[...] = _sigmoid(seg(5)).astype(BF16)
    q_head(4)
    kv_head(4)
    sga_ref[...] = _sigmoid(seg(6)).astype(BF16)
    q_head(5)
    kv_head(5)
    q_head(6)
    kv_head(6)
    q_head(7)
    kv_head(7)
    xcm = seg(0)
    xcm_ref[...] = xcm.astype(BF16)
    tail_ref[0] = xcm[xcm.shape[0] - SUBLANES:, :]
    vm_ref[...] = seg(1).astype(BF16)


def _kv_head(hd, ckv_b, kr_aug, w_ref, gkn_ref, k_ref, v_ref):
    kv = _dot(ckv_b, w_ref[:, hd * 2 * LANES:(hd + 1) * 2 * LANES])
    kn = kv[:, :NOPE]
    k_ref[0, hd, :, :NOPE] = ((kn * _rms_scale(kn, NOPE)) * gkn_ref[...]).astype(BF16)
    k_ref[0, hd, :, NOPE:] = kr_aug
    lane = lax.broadcasted_iota(jnp.int32, kv.shape, 1)
    v_aug = jnp.where(lane < V_DIM, pltpu.roll(kv, V_DIM, axis=1), (lane == V_DIM).astype(F32))
    v_ref[0, hd] = v_aug.astype(BF16)


def _inproj(x, prm, cos_t, sin_t, negb, tm, emit_kv):
    B, L, _ = x.shape
    T = B * L
    tps = L // tm
    x2 = x.reshape(T, D_MODEL)
    row = lambda w: pl.BlockSpec((tm, w), lambda i: (i, 0))
    pos = pl.BlockSpec((tm, LANES), lambda i: (i % tps, 0))
    head_rows = lambda w: pl.BlockSpec((1, H_A, tm, w), lambda i: (i // tps, 0, i % tps, 0))
    consts = [prm["norm_g"], prm["w_cq"], prm["w_small"], prm["qn_g"], prm["w_uq"],
              prm["kvn_g"], prm["g_kr"], prm["g_kr2"], prm["g_qn"], prm["g_qr"], prm["g_qr2"],
              prm["w_ukv"], prm["g_kn"]] + list(prm["w_wide"])
    wide = jax.ShapeDtypeStruct((T, D_MODEL), BF16)
    heads = lambda w: jax.ShapeDtypeStruct((B, H_A, L, w), BF16)
    out_shape = [wide] * 6 + [
        jax.ShapeDtypeStruct((T, LANES), F32),
        jax.ShapeDtypeStruct((T, KV_LORA), F32),
        jax.ShapeDtypeStruct((T, ROPE), F32),
        heads(QK_PAD),
        jax.ShapeDtypeStruct((B, SUBLANES, D_MLSTM), F32),
    ] + ([heads(QK_PAD), heads(V_PAD)] if emit_kv else [])
    out_specs = [row(D_MODEL)] * 6 + [
        row(LANES), row(KV_LORA), row(ROPE), head_rows(QK_PAD),
        pl.BlockSpec((1, SUBLANES, D_MLSTM), lambda i: (i // tps, 0, 0)),
    ] + ([head_rows(QK_PAD), head_rows(V_PAD)] if emit_kv else [])
    return pl.pallas_call(
        functools.partial(_inproj_kernel, emit_kv=emit_kv),
        grid=(T // tm,),
        in_specs=[row(D_MODEL)] + [_const_spec(c.shape) for c in consts] + [
            pos, pos, pl.BlockSpec((1, 1, LANES), lambda i: (i // tps, 0, 0))],
        out_specs=out_specs,
        out_shape=out_shape,
        compiler_params=pltpu.CompilerParams(
            dimension_semantics=("arbitrary",), vmem_limit_bytes=VMEM_LIMIT_BYTES),
        name="inproj",
    )(x2, *consts, cos_t, sin_t, negb)


def _kvup_kernel(ckv_ref, kr_ref, w_ref, gkn_ref, k_ref, v_ref):
    ckv_b = ckv_ref[0].astype(BF16)
    kr_aug = kr_ref[0].astype(BF16)
    for hd in range(H_A):
        _kv_head(hd, ckv_b, kr_aug, w_ref, gkn_ref, k_ref, v_ref)


def _kvup(ckv, kr, prm, tm):
    B, T, _ = ckv.shape
    return pl.pallas_call(
        _kvup_kernel,
        grid=(B, T // tm),
        in_specs=[pl.BlockSpec((1, tm, KV_LORA), lambda b, i: (b, i, 0)),
                  pl.BlockSpec((1, tm, LANES), lambda b, i: (b, i, 0)),
                  _const_spec(prm["w_ukv"].shape), _const_spec(prm["g_kn"].shape)],
        out_specs=[pl.BlockSpec((1, H_A, tm, QK_PAD), lambda b, i: (b, 0, i, 0)),
                   pl.BlockSpec((1, H_A, tm, V_PAD), lambda b, i: (b, 0, i, 0))],
        out_shape=[jax.ShapeDtypeStruct((B, H_A, T, QK_PAD), BF16),
                   jax.ShapeDtypeStruct((B, H_A, T, V_PAD), BF16)],
        compiler_params=pltpu.CompilerParams(
            dimension_semantics=("arbitrary", "arbitrary"), vmem_limit_bytes=VMEM_LIMIT_BYTES),
        name="kvup",
    )(ckv, kr, prm["w_ukv"], prm["g_kn"])


HEADS_PER_STEP = 4


def _attn_kernel(safe_ref, q_ref, k_ref, v_ref, o_ref, acc_ref, m_ref, *, tq, wide, past):
    b, qi = pl.program_id(0), pl.program_id(2)
    diag0 = past + qi * tq
    heads = range(HEADS_PER_STEP)

    def chunk_mask(width):
        rows = lax.broadcasted_iota(jnp.int32, (tq, width), 0) // CHUNK
        cols = lax.broadcasted_iota(jnp.int32, (tq, width), 1) // CHUNK
        return cols <= rows + (width - tq) // CHUNK

    @pl.when(safe_ref[b] == 1)
    def _bounded():
        def step(ks, size, msk, first):
            scores = [_dot_nt(q_ref[0, h], k_ref[0, h, pl.ds(ks, size), :]) for h in heads]
            for h in heads:
                s = scores[h] if msk is None else jnp.where(msk, scores[h], NEG)
                pv = _dot(jnp.exp2(s).astype(BF16), v_ref[0, h, pl.ds(ks, size), :])
                if first:
                    acc_ref[h] = pv
                else:
                    acc_ref[h] += pv

        def diag_step(ks, before):
            if tq % (2 * CHUNK):
                step(ks, before + tq, chunk_mask(before + tq), True)
                return
            half = tq // 2
            for r0, nkeys in ((0, before + half), (half, before + tq)):
                rows = slice(r0, r0 + half)
                rr = lax.broadcasted_iota(jnp.int32, (half, nkeys), 0) // CHUNK
                cc = lax.broadcasted_iota(jnp.int32, (half, nkeys), 1) // CHUNK
                msk = cc <= rr + (before + r0) // CHUNK
                scores = [_dot_nt(q_ref[0, h, rows, :], k_ref[0, h, pl.ds(ks, nkeys), :])
                          for h in heads]
                for h in heads:
                    p = jnp.exp2(jnp.where(msk, scores[h], NEG)).astype(BF16)
                    acc_ref[h, rows, :] = _dot(p, v_ref[0, h, pl.ds(ks, nkeys), :])

        @pl.when(diag0 % wide == 0)
        def _():
            diag_step(pl.multiple_of(diag0, tq), 0)

        @pl.when(diag0 % wide != 0)
        def _():
            diag_step(pl.multiple_of(diag0 - tq, tq), tq)

        @pl.loop(0, diag0 // wide)
        def _(j):
            step(pl.multiple_of(j * wide, wide), wide, None, False)

    @pl.when(safe_ref[b] == 0)
    def _online():
        acc_ref[...] = jnp.zeros_like(acc_ref)
        m_ref[...] = jnp.full_like(m_ref, NEG)

        def step(ks, msk):
            for h in heads:
                s = _dot_nt(q_ref[0, h], k_ref[0, h, pl.ds(ks, tq), :])
                if msk is not None:
                    s = jnp.where(msk, s, NEG)
                m_old = m_ref[h]
                m_new = jnp.maximum(m_old, jnp.max(s, axis=1, keepdims=True))
                p = jnp.exp2(s - m_new).astype(BF16)
                acc_ref[h] = (jnp.exp2(m_old - m_new) * acc_ref[h]
                              + _dot(p, v_ref[0, h, pl.ds(ks, tq), :]))
                m_ref[h] = m_new

        @pl.loop(0, diag0 // tq)
        def _(j):
            step(pl.multiple_of(j * tq, tq), None)

        step(pl.multiple_of(diag0, tq), chunk_mask(tq))

    for h in heads:
        acc = acc_ref[h]
        o_ref[0, :, h * V_DIM:(h + 1) * V_DIM] = (
            acc[:, :V_DIM] / acc[:, V_DIM:V_DIM + 1]).astype(BF16)


def _attention(q, k, v, safe, tq, wide):
    B, H, L, _ = q.shape
    T = k.shape[2]
    past = T - L
    hp = HEADS_PER_STEP
    assert past % wide == 0 and (wide == 2 * tq or L == tq) and L % tq == 0 and H % hp == 0
    grid_spec = pltpu.PrefetchScalarGridSpec(
        num_scalar_prefetch=1,
        grid=(B, H // hp, L // tq),
        in_specs=[pl.BlockSpec((1, hp, tq, QK_PAD), lambda b, h, i, s: (b, h, i, 0)),
                  pl.BlockSpec((1, hp, T, QK_PAD), lambda b, h, i, s: (b, h, 0, 0)),
                  pl.BlockSpec((1, hp, T, V_PAD), lambda b, h, i, s: (b, h, 0, 0))],
        out_specs=pl.BlockSpec((1, tq, hp * V_DIM), lambda b, h, i, s: (b, i, h)),
        scratch_shapes=[pltpu.VMEM((hp, tq, V_PAD), F32), pltpu.VMEM((hp, tq, 1), F32)],
    )
    return pl.pallas_call(
        functools.partial(_attn_kernel, tq=tq, wide=wide, past=past),
        grid_spec=grid_spec,
        out_shape=jax.ShapeDtypeStruct((B, L, D_ATT), BF16),
        compiler_params=pltpu.CompilerParams(
            dimension_semantics=("arbitrary", "arbitrary", "arbitrary"),
            vmem_limit_bytes=VMEM_LIMIT_BYTES),
        name="attention",
    )(safe, q, k, v)


def _mlstm_kernel(xcm_ref, vm_ref, gates_ref, og_ref, sgm_ref, o_ref, sza_ref, sga_ref, x_ref,
                  cbuf_ref, c0_ref, n0_ref, m0_ref, bif_ref, cw_ref, cb_ref, wq_ref, wkt_ref,
                  hg_ref, wpm_ref, wpa_ref, wout_ref, tri_ref,
                  y_ref, c_ref, n_ref, m_ref, xpad_ref, hn_ref, *, ch):
    ci = pl.program_id(1)

    @pl.when(ci == 0)
    def _():
        c_ref[...] = c0_ref[...]
        n_ref[...] = n0_ref[...]
        m_ref[...] = m0_ref[...]
        xpad_ref[0:SUBLANES, :] = cbuf_ref[0]

    pa = _dot(o_ref[0] * sza_ref[0], wpa_ref[...])

    xpad_ref[SUBLANES:, :] = xcm_ref[0].astype(F32)

    gpre = gates_ref[0] + bif_ref[...]
    lane = lax.broadcasted_iota(jnp.int32, (ch, LANES), 1)
    lf = jnp.minimum(gpre, 0.0) - jnp.log1p(jnp.exp(-jnp.abs(gpre)))
    lf = jnp.where((lane >= H_M) & (lane < 2 * H_M), lf, 0.0)
    tri = (lax.broadcasted_iota(jnp.int32, (ch, ch), 1)
           <= lax.broadcasted_iota(jnp.int32, (ch, ch), 0))
    bcum = sum(_dot(tri_ref[...], part) for part in _split3(lf))
    xmat = jnp.where(lane < H_M, gpre, -bcum)
    if ch % LANES:
        xmat = jnp.concatenate([xmat, jnp.zeros((LANES - ch % LANES, LANES), F32)], axis=0)
    xrow = xmat.T

    def project(hd):
        sl = slice(hd * DH_M, (hd + 1) * DH_M)
        u = cb_ref[:, sl] + xpad_ref[SUBLANES:, sl] * cw_ref[CONV_W - 1:CONV_W, sl]
        for j in range(CONV_W - 1):
            off = SUBLANES - (CONV_W - 1) + j
            u = u + xpad_ref[off:off + ch, sl] * cw_ref[j:j + 1, sl]
        su_h = (u * _sigmoid(u)).astype(BF16)
        qh = _dot(su_h, wq_ref[hd])
        kt = _dot_nt(wkt_ref[hd], su_h)
        qb = qh.astype(BF16)
        c0 = c_ref[0, hd]
        return qh, kt, _dot(qb, kt.astype(BF16)), _dot(qb, c0.astype(BF16)), c0

    def recur(hd, qh, kt, s_raw, q_c0, c0):
        sl = slice(hd * DH_M, (hd + 1) * DH_M)
        m0 = m_ref[0, hd:hd + 1, 0:1]
        bc = bcum[:, H_M + hd:H_M + hd + 1]
        drow = xrow[hd:hd + 1, :ch] + xrow[H_M + hd:H_M + hd + 1, :ch]
        dlog = jnp.where(tri, bc + drow, NEG)
        g = bc + m0
        m = jnp.maximum(g, jnp.max(dlog, axis=1, keepdims=True))
        w_intra = jnp.exp(dlog - m)
        w_inter = jnp.exp(g - m)

        v = vm_ref[0, :, sl]
        n0 = n_ref[0, hd:hd + 1, :]
        s = s_raw * w_intra
        num = w_inter * q_c0 + _dot(s.astype(BF16), v)
        den = (w_inter * jnp.sum(qh * n0, axis=1, keepdims=True)
               + jnp.sum(s, axis=1, keepdims=True))
        hh = num / jnp.maximum(jnp.abs(den), jnp.exp(-m))
        hn_ref[:, sl] = ((hh * _rms_scale(hh, DH_M)) * hg_ref[:, sl]).astype(BF16)

        m_last = m[ch - 1:ch, :]
        b_last = bc[ch - 1:ch, :]
        w_state = jnp.exp(drow + (b_last - m_last))
        decay = jnp.exp(b_last + m0 - m_last)
        c_ref[0, hd] = decay * c0 + _dot((kt * w_state).astype(BF16), v)
        w8 = jnp.broadcast_to(w_state, (SUBLANES, ch)).astype(BF16)
        n_ref[0, hd:hd + 1, :] = decay * n0 + _dot_nt(w8, kt.astype(BF16))[0:1, :]
        m_ref[0, hd:hd + 1, :] = jnp.broadcast_to(m_last, (1, LANES))

    pending = project(0)
    for hd in range(H_M):
        ahead = project(hd + 1) if hd + 1 < H_M else None
        recur(hd, *pending)
        pending = ahead
    xpad_ref[0:SUBLANES, :] = xpad_ref[ch:ch + SUBLANES, :]

    hm = hn_ref[...] * og_ref[0]
    pm = _dot(hm, wpm_ref[...])
    merged = sgm_ref[0] * pm.astype(BF16) + sga_ref[0] * pa.astype(BF16)
    y_ref[0] = x_ref[0] + _dot(merged, wout_ref[...])


def _mlstm(xcm, vm, gates, og, sgm, o, sza, sga, x, cbuf, c0, n0, m0b, prm, ch):
    B, L, _ = xcm.shape
    seq = lambda w: pl.BlockSpec((1, ch, w), lambda b, c: (b, c, 0))
    per_b = lambda *s: pl.BlockSpec((1,) + s, lambda b, c: (b,) + (0,) * len(s))
    consts = [prm["b_if"], prm["conv_w"], prm["conv_b"], prm["wq_m"], prm["wk_m"], prm["hnorm_g"],
              prm["w_pm"], prm["w_pa"], prm["w_out"],
              jnp.tril(jnp.ones((ch, ch), BF16))]
    return pl.pallas_call(
        functools.partial(_mlstm_kernel, ch=ch),
        grid=(B, L // ch),
        in_specs=[seq(D_MLSTM), seq(D_MLSTM), seq(LANES), seq(D_MLSTM), seq(D_MODEL),
                  seq(D_ATT), seq(D_ATT), seq(D_MODEL), seq(D_MODEL),
                  per_b(SUBLANES, D_MLSTM), per_b(H_M, DH_M, DH_M), per_b(H_M, DH_M),
                  per_b(H_M, LANES)] + [_const_spec(c.shape) for c in consts],
        out_specs=[seq(D_MODEL), per_b(H_M, DH_M, DH_M), per_b(H_M, DH_M), per_b(H_M, LANES)],
        out_shape=[jax.ShapeDtypeStruct((B, L, D_MODEL), F32),
                   jax.ShapeDtypeStruct((B, H_M, DH_M, DH_M), F32),
                   jax.ShapeDtypeStruct((B, H_M, DH_M), F32),
                   jax.ShapeDtypeStruct((B, H_M, LANES), F32)],
        scratch_shapes=[pltpu.VMEM((ch + SUBLANES, D_MLSTM), F32),
                        pltpu.VMEM((ch, D_MLSTM), BF16)],
        compiler_params=pltpu.CompilerParams(
            dimension_semantics=("arbitrary", "arbitrary"), vmem_limit_bytes=VMEM_LIMIT_BYTES),
        name="mlstm",
    )(xcm, vm, gates, og, sgm, o, sza, sga, x, cbuf, c0, n0, m0b, *consts)


def _swap_halves(a):
    half = ROPE // 2
    return jnp.concatenate([a[..., half:], a[..., :half]], axis=-1)


def _prepare_params(norm_g, w_in, b_if, conv_w, conv_b, wq_m, wk_m, hnorm_g, qn_g, w_uq, kvn_g,
                    w_ukv, g_qn, g_qr, g_kn, g_kr, w_pm, w_pa, w_out):
    cols, start = [], 0
    for w in SPLITS:
        cols.append(w_in[:, start:start + w])
        start += w
    xcm, vm, ig, fg, og, zm, cq, ckv, kr, za, gm, ga = cols
    pad = jnp.zeros((D_MODEL, LANES - 2 * H_M), F32)
    w_small = jnp.concatenate([ckv, kr, _swap_halves(kr), ig, fg, pad], axis=1)
    wuq = w_uq.reshape(Q_LORA, H_A, QK_DIM)
    wuq = jnp.concatenate([wuq, _swap_halves(wuq[..., NOPE:])], axis=-1)
    row = lambda a: a.reshape(1, -1).astype(F32)
    pair = lambda g: row(jnp.concatenate([g, _swap_halves(g)]))
    pair2 = lambda g: row(jnp.concatenate([_swap_halves(g), g]))
    return {
        "norm_g": row(norm_g),
        "w_wide": [w.astype(BF16) for w in (xcm, vm, og, zm, za, gm, ga)],
        "w_cq": cq.astype(BF16),
        "w_small": w_small.astype(BF16),
        "qn_g": row(qn_g),
        "w_uq": wuq.reshape(Q_LORA, H_A * 2 * LANES).astype(BF16),
        "kvn_g": row(kvn_g),
        "g_kr": pair(g_kr), "g_kr2": pair2(g_kr),
        "g_qn": row(g_qn),
        "g_qr": pair(g_qr), "g_qr2": pair2(g_qr),
        "w_ukv": w_ukv.astype(BF16),
        "g_kn": row(g_kn),
        "q_bound2": (ATT_SCALE * LOG2E) ** 2 * (NOPE * jnp.max(g_qn * g_qn)
                                                + ROPE * jnp.max(g_qr * g_qr)),
        "kn_bound2": NOPE * jnp.max(g_kn * g_kn),
        "kr_bound2": ROPE * jnp.max(g_kr * g_kr),
        "b_if": row(jnp.concatenate([b_if, jnp.zeros((LANES - 2 * H_M,), F32)])),
        "conv_w": conv_w, "conv_b": row(conv_b),
        "wq_m": wq_m.astype(BF16),
        "wk_m": (jnp.swapaxes(wk_m, 1, 2) * (DH_M ** -0.5)).astype(BF16),
        "hnorm_g": row(hnorm_g),
        "w_pm": w_pm.astype(BF16), "w_pa": w_pa.astype(BF16), "w_out": w_out.astype(BF16),
    }


def _rope_tables(past, length):
    half = ROPE // 2
    inv = jnp.power(ROPE_THETA, -jnp.arange(half, dtype=F32) / half)
    ang = (past + jnp.arange(length)).astype(F32)[:, None] * inv[None, :]
    cos, sin = jnp.cos(ang), jnp.sin(ang)
    return jnp.tile(cos, (1, 4)), jnp.tile(jnp.concatenate([-sin, sin], axis=1), (1, 2))


def _hybrid_layer(x, past_ckv, past_kr, conv_buf, c0, n0, m0, prm, *, tm, tq, wide, ch):
    B, L, _ = x.shape
    past = past_ckv.shape[1]
    cos_t, sin_t = _rope_tables(past, L)
    seq = lambda a: a.reshape(B, L, a.shape[-1])

    kr2 = jnp.broadcast_to(prm["kr_bound2"], (B,))
    if past:
        pk = past_kr.astype(F32)
        kr2 = jnp.maximum(kr2, jnp.max(jnp.sum(pk * pk, axis=-1), axis=1))
    bound = jnp.sqrt((prm["kn_bound2"] + kr2 + 1.0) * prm["q_bound2"]) * 1.01
    safe = bound <= SAFE_BOUND
    negb = jnp.broadcast_to(jnp.where(safe, -bound, 0.0)[:, None, None], (B, 1, LANES))

    outs = _inproj(x, prm, cos_t, sin_t, negb, tm, emit_kv=not past)
    xcm, vm, og, sza, sgm, sga, gates, ckv_new, kr_new, q, tail = outs[:11]
    ckv_new, kr_new = seq(ckv_new), seq(kr_new)
    if past:
        T = past + L
        aug = jnp.concatenate(
            [jnp.ones((B, T, 1), F32), jnp.zeros((B, T, LANES - ROPE - 1), F32)], axis=-1)
        kr_all = jnp.concatenate([jnp.concatenate([past_kr, kr_new], axis=1), aug], axis=-1)
        k, v = _kvup(jnp.concatenate([past_ckv, ckv_new], axis=1), kr_all, prm, T)
    else:
        k, v = outs[11:]
    o = _attention(q, k, v, safe.astype(jnp.int32), tq, wide)

    cbuf = jnp.concatenate(
        [jnp.zeros((B, SUBLANES - (CONV_W - 1), D_MLSTM), F32), conv_buf.astype(F32)], axis=1)
    m0b = jnp.broadcast_to(m0.astype(F32)[:, :, None], (B, H_M, LANES))
    y, c1, n1, m1 = _mlstm(seq(xcm), seq(vm), seq(gates), seq(og), seq(sgm), o, seq(sza), seq(sga),
                           x, cbuf, c0.astype(F32), n0.astype(F32), m0b, prm, ch)
    conv_new = tail[:, SUBLANES - (CONV_W - 1):, :]
    return y, ckv_new, kr_new, conv_new, c1, n1, m1[:, :, 0]


def kernel(x_prompt, x_sample, cache_ckv, cache_kr, state_conv, state_C, state_n, state_m,
           norm_g, w_in, b_if, conv_w, conv_b, wq_m, wk_m, hnorm_g,
           qn_g, w_uq, kvn_g, w_ukv, g_qn, g_qr, g_kn, g_kr, w_pm, w_pa, w_out):
    dt = x_prompt.dtype
    B = x_prompt.shape[0]
    depth = norm_g.shape[0]
    yp, ys = x_prompt, x_sample
    outs_p = [[] for _ in range(6)]
    outs_s = [[] for _ in range(6)]
    for l in range(depth):
        prm = _prepare_params(norm_g[l], w_in[l], b_if[l], conv_w[l], conv_b[l], wq_m[l], wk_m[l],
                              hnorm_g[l], qn_g[l], w_uq[l], kvn_g[l], w_ukv[l], g_qn[l], g_qr[l],
                              g_kn[l], g_kr[l], w_pm[l], w_pa[l], w_out[l])
        yp, *res_p = _hybrid_layer(
            yp, jnp.zeros((B, 0, KV_LORA), dt), jnp.zeros((B, 0, ROPE), dt),
            jnp.zeros((B, CONV_W - 1, D_MLSTM), dt), jnp.zeros((B, H_M, DH_M, DH_M), F32),
            jnp.zeros((B, H_M, DH_M), F32), jnp.zeros((B, H_M), F32), prm,
            tm=512, tq=512, wide=1024, ch=256)
        Ls = x_sample.shape[1]
        ys, *res_s = _hybrid_layer(
            ys, cache_ckv[l], cache_kr[l], state_conv[l], state_C[l], state_n[l], state_m[l], prm,
            tm=Ls, tq=Ls, wide=cache_ckv.shape[2], ch=Ls)
        for lst, a in zip(outs_p, res_p):
            lst.append(a.astype(dt))
        for lst, a in zip(outs_s, res_s):
            lst.append(a.astype(dt))
    return (yp, ys) + tuple(jnp.stack(a) for a in outs_p) + tuple(jnp.stack(a) for a in outs_s)
```
